```python
import jax, jax.numpy as jnp
from jax import lax
import numpy as np

D_MODEL = 1024
BATCH = 8
SEQ = 4096
DEPTH = 1

N_MEM = 256
CONV_WIDTH = D_MODEL // 2
CONV_KERNEL = 31
RET_WIDTH = D_MODEL - CONV_WIDTH
RET_HEADS = 4
RET_HEAD_DIM = RET_WIDTH // RET_HEADS
RET_CHUNK = 128
ROPE_BASE = 10000.0
IN_COLS = 2 * CONV_WIDTH + 4 * RET_WIDTH
SPLITS = (CONV_WIDTH, 2 * CONV_WIDTH, 2 * CONV_WIDTH + RET_WIDTH,
          2 * CONV_WIDTH + 2 * RET_WIDTH, 2 * CONV_WIDTH + 3 * RET_WIDTH)
XATTN_HEADS = 4
XATTN_HEAD_DIM = D_MODEL // XATTN_HEADS
D_FF = 4 * D_MODEL
LN_EPS = 1e-5
DN_ALPHA = (2.0 * DEPTH) ** 0.25
DN_BETA = (8.0 * DEPTH) ** -0.25

kernel_name = "deepnorm_conformer_retention_hybrid"


def layer_norm(x, g, b):
    xf = x.astype(jnp.float32)
    mu = jnp.mean(xf, axis=-1, keepdims=True)
    var = jnp.mean(jnp.square(xf - mu), axis=-1, keepdims=True)
    y = (xf - mu) * lax.rsqrt(var + LN_EPS)
    return (y * g.astype(jnp.float32) + b.astype(jnp.float32)).astype(x.dtype)


def rotary(t, pos):
    half = t.shape[-1] // 2
    inv = ROPE_BASE ** (-jnp.linspace(0.0, 1.0, half, dtype=jnp.float32))
    ang = pos[:, None] * inv[None, :]
    cos = jnp.cos(ang)[None, :, None, :].astype(t.dtype)
    sin = jnp.sin(ang)[None, :, None, :].astype(t.dtype)
    t1, t2 = t[..., :half], t[..., half:]
    return jnp.concatenate([t1 * cos - t2 * sin, t2 * cos + t1 * sin], axis=-1)


def retention_chunkwise(q, k, v):
    B, S, H, d = q.shape
    C = RET_CHUNK
    n_chunks = S // C
    log_g = jnp.log(1.0 - 2.0 ** (-5.0 - jnp.arange(H, dtype=jnp.float32)))
    idx = jnp.arange(C, dtype=jnp.float32)
    rel = idx[:, None] - idx[None, :]
    decay = jnp.where(rel >= 0, jnp.exp(log_g[:, None, None] * jnp.maximum(rel, 0.0)), 0.0)
    q_dec = jnp.exp(log_g[:, None] * (idx + 1.0))
    k_dec = jnp.exp(log_g[:, None] * (C - 1.0 - idx))
    chunk_dec = jnp.exp(log_g * C)

    def to_chunks(t):
        return t.reshape(B, n_chunks, C, H, d).transpose(1, 0, 3, 2, 4)

    def step(state, qkv):
        qc, kc, vc = qkv
        scores = jnp.einsum('bhnd,bhmd->bhnm', qc, kc) * decay[None]
        intra = jnp.einsum('bhnm,bhmd->bhnd', scores, vc)
        cross = jnp.einsum('bhnd,bhde->bhne', qc * q_dec[None, :, :, None], state)
        new_state = state * chunk_dec[None, :, None, None] + jnp.einsum(
            'bhmd,bhme->bhde', kc * k_dec[None, :, :, None], vc)
        return new_state, intra + cross

    state0 = jnp.zeros((B, H, d, d), jnp.float32)
    _, out = lax.scan(step, state0, (to_chunks(q), to_chunks(k), to_chunks(v)))
    return out.transpose(1, 0, 3, 2, 4).reshape(B, S, H, d)


def hybrid_mixer(x, w_in, conv_w, conv_b, conv_ln_g, conv_ln_b, ret_gn_g, ret_gn_b, w_out):
    B, S, _ = x.shape
    h = x @ w_in
    a, b, q, k, v, g = jnp.split(h, SPLITS, axis=-1)
    u = a * jax.nn.sigmoid(b)
    u = lax.conv_general_dilated(
        u, conv_w[:, None, :].astype(u.dtype), window_strides=(1,),
        padding=[(CONV_KERNEL - 1, 0)], dimension_numbers=('NWC', 'WIO', 'NWC'),
        feature_group_count=CONV_WIDTH) + conv_b
    conv_out = jax.nn.silu(layer_norm(u, conv_ln_g, conv_ln_b))
    pos = jnp.arange(S, dtype=jnp.float32)
    q = rotary(q.reshape(B, S, RET_HEADS, RET_HEAD_DIM), pos) * (RET_HEAD_DIM ** -0.5)
    k = rotary(k.reshape(B, S, RET_HEADS, RET_HEAD_DIM), pos)
    v = v.reshape(B, S, RET_HEADS, RET_HEAD_DIM)
    y = retention_chunkwise(q.astype(jnp.float32), k.astype(jnp.float32), v.astype(jnp.float32))
    mu = jnp.mean(y, axis=-1, keepdims=True)
    var = jnp.mean(jnp.square(y - mu), axis=-1, keepdims=True)
    y = ((y - mu) * lax.rsqrt(var + LN_EPS)).reshape(B, S, RET_WIDTH)
    y = y * ret_gn_g.astype(jnp.float32) + ret_gn_b.astype(jnp.float32)
    ret_out = jax.nn.silu(g) * y.astype(x.dtype)
    return jnp.concatenate([conv_out, ret_out], axis=-1) @ w_out


def memory_cross_attention(x, mem, w_xq, w_xk, w_xv, w_xo):
    B, S, D = x.shape
    M = mem.shape[1]
    q = (x @ w_xq).reshape(B, S, XATTN_HEADS, XATTN_HEAD_DIM)
    k = (mem @ w_xk).reshape(B, M, XATTN_HEADS, XATTN_HEAD_DIM)
    v = (mem @ w_xv).reshape(B, M, XATTN_HEADS, XATTN_HEAD_DIM)
    s = jnp.einsum('bshd,bmhd->bhsm', q, k).astype(jnp.float32) * (XATTN_HEAD_DIM ** -0.5)
    p = jax.nn.softmax(s, axis=-1).astype(x.dtype)
    o = jnp.einsum('bhsm,bmhd->bshd', p, v).reshape(B, S, D)
    return o @ w_xo


def squared_relu_mlp(x, w_up, w_down):
    return jnp.square(jax.nn.relu(x @ w_up)) @ w_down


def setup_inputs(seed: int = 0) -> dict:
    key = jax.random.key(seed)
    ks = jax.random.split(key, 22)
    f32 = jnp.float32
    L, D = DEPTH, D_MODEL

    def dense(k, shape, fan_in, scale=1.0):
        return jax.random.normal(k, shape, f32) * (scale * fan_in ** -0.5)

    def gain(k, shape):
        return 1.0 + 0.02 * jax.random.normal(k, shape, f32)

    def bias(k, shape):
        return 0.02 * jax.random.normal(k, shape, f32)

    return {
        'x': jax.random.normal(ks[0], (BATCH, SEQ, D), f32),
        'mem': jax.random.normal(ks[1], (BATCH, N_MEM, D), f32),
        'w_in': dense(ks[2], (L, D, IN_COLS), D),
        'conv_w': dense(ks[3], (L, CONV_KERNEL, CONV_WIDTH), CONV_KERNEL),
        'conv_b': bias(ks[4], (L, CONV_WIDTH)),
        'conv_ln_g': gain(ks[5], (L, CONV_WIDTH)),
        'conv_ln_b': bias(ks[6], (L, CONV_WIDTH)),
        'ret_gn_g': gain(ks[7], (L, RET_WIDTH)),
        'ret_gn_b': bias(ks[8], (L, RET_WIDTH)),
        'w_out': dense(ks[9], (L, D, D), D, DN_BETA),
        'ln1_g': gain(ks[10], (L, D)),
        'ln1_b': bias(ks[11], (L, D)),
        'w_xq': dense(ks[12], (L, D, D), D),
        'w_xk': dense(ks[13], (L, D, D), D),
        'w_xv': dense(ks[14], (L, D, D), D, DN_BETA),
        'w_xo': dense(ks[15], (L, D, D), D, DN_BETA),
        'ln2_g': gain(ks[16], (L, D)),
        'ln2_b': bias(ks[17], (L, D)),
        'w_up': dense(ks[18], (L, D, D_FF), D, DN_BETA),
        'w_down': dense(ks[19], (L, D_FF, D), D_FF, DN_BETA),
        'ln3_g': gain(ks[20], (L, D)),
        'ln3_b': bias(ks[21], (L, D)),
    }


def reference(x, mem, w_in, conv_w, conv_b, conv_ln_g, conv_ln_b, ret_gn_g, ret_gn_b, w_out,
              ln1_g, ln1_b, w_xq, w_xk, w_xv, w_xo, ln2_g, ln2_b, w_up, w_down, ln3_g, ln3_b):
    for l in range(DEPTH):
        mix = hybrid_mixer(x, w_in[l], conv_w[l], conv_b[l], conv_ln_g[l], conv_ln_b[l],
                           ret_gn_g[l], ret_gn_b[l], w_out[l])
        x = layer_norm(DN_ALPHA * x + mix, ln1_g[l], ln1_b[l])
        xa = memory_cross_attention(x, mem, w_xq[l], w_xk[l], w_xv[l], w_xo[l])
        x = layer_norm(DN_ALPHA * x + xa, ln2_g[l], ln2_b[l])
        ff = squared_relu_mlp(x, w_up[l], w_down[l])
        x = layer_norm(DN_ALPHA * x + ff, ln3_g[l], ln3_b[l])
    return x
```

```python
import functools

import jax
import jax.numpy as jnp
from jax import lax
from jax.experimental import pallas as pl
from jax.experimental.pallas import tpu as pltpu

D_MODEL = 1024
N_MEM = 256
CONV_WIDTH = 512
CONV_KERNEL = 31
RET_WIDTH = 512
RET_HEADS = 4
RET_HEAD_DIM = 128
RET_CHUNK = 128
ROPE_BASE = 10000.0
IN_COLS = 2 * CONV_WIDTH + 4 * RET_WIDTH
XATTN_HEADS = 4
XATTN_HEAD_DIM = 256
D_FF = 4 * D_MODEL
LN_EPS = 1e-5
DEPTH = 1
DN_ALPHA = (2.0 * DEPTH) ** 0.25

CONV_HALO = 32
MIXER_TILE = 512
ATTN_TILE = 512
FF_CHUNK = 1024
VMEM_LIMIT_BYTES = 56 * 1024 * 1024

BF16 = jnp.bfloat16
F32 = jnp.float32


def _dot(a, b):
    return jnp.dot(a, b, preferred_element_type=F32)


def _dot_nt(a, b):
    return lax.dot_general(a, b, (((1,), (1,)), ((), ())), preferred_element_type=F32)


def _sigmoid(x):
    return 1.0 / (1.0 + jnp.exp(-x))


def _layer_norm(x, g, b):
    mu = jnp.mean(x, axis=-1, keepdims=True)
    xc = x - mu
    var = jnp.mean(xc * xc, axis=-1, keepdims=True)
    return xc * lax.rsqrt(var + LN_EPS) * g + b


def _kv_kernel(mem_ref, wk_ref, wv_ref, k_ref, v_ref):
    m = mem_ref[0].astype(BF16)
    k_ref[0] = _dot(m, wk_ref[...]).astype(BF16)
    v_ref[0] = _dot(m, wv_ref[...]).astype(BF16)


def _kv_proj(mem, w_xk, w_xv):
    B = mem.shape[0]
    const = lambda b: (0, 0)
    return pl.pallas_call(
        _kv_kernel,
        grid=(B,),
        in_specs=[
            pl.BlockSpec((1, N_MEM, D_MODEL), lambda b: (b, 0, 0)),
            pl.BlockSpec((D_MODEL, D_MODEL), const),
            pl.BlockSpec((D_MODEL, D_MODEL), const),
        ],
        out_specs=[
            pl.BlockSpec((1, N_MEM, D_MODEL), lambda b: (b, 0, 0)),
            pl.BlockSpec((1, N_MEM, D_MODEL), lambda b: (b, 0, 0)),
        ],
        out_shape=[jax.ShapeDtypeStruct((B, N_MEM, D_MODEL), BF16)] * 2,
        compiler_params=pltpu.CompilerParams(dimension_semantics=("arbitrary",),
                                             vmem_limit_bytes=VMEM_LIMIT_BYTES),
        name="kv_proj",
    )(mem, w_xk, w_xv)


def _mixer_kernel(x_ref, w_in_ref, conv_w_ref, conv_b_ref, cln_g_ref, cln_b_ref, gn_g_ref, gn_b_ref,
                  w_out_ref, ln1_g_ref, ln1_b_ref, cosq_ref, sinq_ref, cosk_ref, sink_ref,
                  decay_ref, qdec_ref, kdec_ref, cdec_ref, o_ref, ubuf_ref, state_ref):
    T = MIXER_TILE
    s_idx = pl.program_id(1)

    @pl.when(s_idx == 0)
    def _():
        ubuf_ref[0:CONV_HALO, :] = jnp.zeros((CONV_HALO, CONV_WIDTH), F32)
        state_ref[...] = jnp.zeros(state_ref.shape, F32)

    x = x_ref[0]
    xb = x.astype(BF16)

    def proj(lo, width):
        return _dot(xb, w_in_ref[:, lo:lo + width])

    a = proj(0, CONV_WIDTH)
    bgate = proj(CONV_WIDTH, CONV_WIDTH)
    ubuf_ref[CONV_HALO:CONV_HALO + T, :] = a * _sigmoid(bgate)
    base = CONV_HALO - (CONV_KERNEL - 1)
    acc = jnp.zeros((T, CONV_WIDTH), F32) + conv_b_ref[...]
    for j in range(CONV_KERNEL):
        acc = acc + ubuf_ref[base + j:base + j + T, :] * conv_w_ref[j:j + 1, :]
    ubuf_ref[0:CONV_HALO, :] = ubuf_ref[T:T + CONV_HALO, :]
    cn = _layer_norm(acc, cln_g_ref[...], cln_b_ref[...])
    conv_out = (cn * _sigmoid(cn)).astype(BF16)

    q_all = proj(2 * CONV_WIDTH, RET_WIDTH)
    k_all = proj(2 * CONV_WIDTH + RET_WIDTH, RET_WIDTH)
    v_all = proj(2 * CONV_WIDTH + 2 * RET_WIDTH, RET_WIDTH)
    g_all = proj(2 * CONV_WIDTH + 3 * RET_WIDTH, RET_WIDTH)
    cosq, sinq = cosq_ref[...], sinq_ref[...]
    cosk, sink = cosk_ref[...], sink_ref[...]
    half = RET_HEAD_DIM // 2
    C = RET_CHUNK
    ret_heads = []
    for h in range(RET_HEADS):
        lo = h * RET_HEAD_DIM
        qh = q_all[:, lo:lo + RET_HEAD_DIM]
        kh = k_all[:, lo:lo + RET_HEAD_DIM]
        vh = v_all[:, lo:lo + RET_HEAD_DIM].astype(BF16)
        qh = qh * cosq + pltpu.roll(qh, half, 1) * sinq
        kh = kh * cosk + pltpu.roll(kh, half, 1) * sink
        state = state_ref[h]
        decay = decay_ref[h]
        qdec = qdec_ref[h]
        kdec = kdec_ref[h]
        cdec = cdec_ref[h]
        ys = []
        for c in range(T // C):
            qc = qh[c * C:(c + 1) * C]
            kc = kh[c * C:(c + 1) * C]
            vc = vh[c * C:(c + 1) * C]
            scores = _dot_nt(qc.astype(BF16), kc.astype(BF16)) * decay
            lhs = jnp.concatenate([scores.astype(BF16), (qc * qdec).astype(BF16)], axis=1)
            rhs = jnp.concatenate([vc, state.astype(BF16)], axis=0)
            ys.append(_dot(lhs, rhs))
            kd_t = (kc * kdec).T.astype(BF16)
            state = state * cdec + _dot(kd_t, vc)
        state_ref[h] = state
        y = jnp.concatenate(ys, axis=0)
        mu = jnp.mean(y, axis=-1, keepdims=True)
        yc = y - mu
        var = jnp.mean(yc * yc, axis=-1, keepdims=True)
        yn = yc * lax.rsqrt(var + LN_EPS)
        yn = yn * gn_g_ref[:, lo:lo + RET_HEAD_DIM] + gn_b_ref[:, lo:lo + RET_HEAD_DIM]
        gh = g_all[:, lo:lo + RET_HEAD_DIM]
        ret_heads.append((gh * _sigmoid(gh) * yn).astype(BF16))

    merged = jnp.concatenate([conv_out] + ret_heads, axis=1)
    mix = _dot(merged, w_out_ref[...])
    o_ref[0] = _layer_norm(DN_ALPHA * x + mix, ln1_g_ref[...], ln1_b_ref[...])


def _mixer(x, w_in, conv_w, conv_b, cln_g, cln_b, gn_g, gn_b, w_out, ln1_g, ln1_b, tabs):
    B, S, D = x.shape
    T = MIXER_TILE
    const2 = lambda b, s: (0, 0)
    const3 = lambda b, s: (0, 0, 0)
    row = lambda n: pl.BlockSpec((1, n), const2)
    pos = lambda: pl.BlockSpec((T, RET_HEAD_DIM), lambda b, s: (s, 0))
    tab = lambda: pl.BlockSpec((RET_HEADS, RET_CHUNK, RET_HEAD_DIM), const3)
    return pl.pallas_call(
        _mixer_kernel,
        grid=(B, S // T),
        in_specs=[
            pl.BlockSpec((1, T, D), lambda b, s: (b, s, 0)),
            pl.BlockSpec((D, IN_COLS), const2),
            pl.BlockSpec((CONV_KERNEL, CONV_WIDTH), const2),
            row(CONV_WIDTH), row(CONV_WIDTH), row(CONV_WIDTH), row(RET_WIDTH), row(RET_WIDTH),
            pl.BlockSpec((D, D), const2),
            row(D), row(D),
            pos(), pos(), pos(), pos(),
            tab(), tab(), tab(), tab(),
        ],
        out_specs=pl.BlockSpec((1, T, D), lambda b, s: (b, s, 0)),
        out_shape=jax.ShapeDtypeStruct((B, S, D), F32),
        scratch_shapes=[
            pltpu.VMEM((CONV_HALO + T, CONV_WIDTH), F32),
            pltpu.VMEM((RET_HEADS, RET_HEAD_DIM, RET_HEAD_DIM), F32),
        ],
        compiler_params=pltpu.CompilerParams(dimension_semantics=("arbitrary", "arbitrary"),
                                             vmem_limit_bytes=VMEM_LIMIT_BYTES),
        name="mixer",
    )(x, w_in, conv_w, conv_b, cln_g, cln_b, gn_g, gn_b, w_out, ln1_g, ln1_b, *tabs)


def _attn_mlp_kernel(x_ref, k_ref, v_ref, wq_ref, wo_ref, ln2_g_ref, ln2_b_ref, wup_ref, wdn_ref,
                     ln3_g_ref, ln3_b_ref, o_ref):
    x = x_ref[0]
    q = (_dot(x.astype(BF16), wq_ref[...]) * (XATTN_HEAD_DIM ** -0.5)).astype(BF16)
    heads = []
    for h in range(XATTN_HEADS):
        lo = h * XATTN_HEAD_DIM
        s = _dot_nt(q[:, lo:lo + XATTN_HEAD_DIM], k_ref[0, :, lo:lo + XATTN_HEAD_DIM])
        p = jnp.exp(s - jnp.max(s, axis=-1, keepdims=True))
        l = jnp.sum(p, axis=-1, keepdims=True)
        o = _dot(p.astype(BF16), v_ref[0, :, lo:lo + XATTN_HEAD_DIM])
        heads.append((o * (1.0 / l)).astype(BF16))
    xa = _dot(jnp.concatenate(heads, axis=1), wo_ref[...])
    x2 = _layer_norm(DN_ALPHA * x + xa, ln2_g_ref[...], ln2_b_ref[...])
    x2b = x2.astype(BF16)
    ff = None
    for j in range(D_FF // FF_CHUNK):
        lo = j * FF_CHUNK
        hmid = jnp.maximum(_dot(x2b, wup_ref[:, lo:lo + FF_CHUNK]), 0.0)
        part = _dot((hmid * hmid).astype(BF16), wdn_ref[lo:lo + FF_CHUNK, :])
        ff = part if ff is None else ff + part
    o_ref[0] = _layer_norm(DN_ALPHA * x2 + ff, ln3_g_ref[...], ln3_b_ref[...])


def _attn_mlp(x1, k, v, w_xq, w_xo, ln2_g, ln2_b, w_up, w_down, ln3_g, ln3_b):
    B, S, D = x1.shape
    T = ATTN_TILE
    const2 = lambda b, s: (0, 0)
    row = lambda n: pl.BlockSpec((1, n), const2)
    weight = lambda shape: pl.BlockSpec(shape, const2, pipeline_mode=pl.Buffered(1))
    return pl.pallas_call(
        _attn_mlp_kernel,
        grid=(B, S // T),
        in_specs=[
            pl.BlockSpec((1, T, D), lambda b, s: (b, s, 0)),
            pl.BlockSpec((1, N_MEM, D), lambda b, s: (b, 0, 0)),
            pl.BlockSpec((1, N_MEM, D), lambda b, s: (b, 0, 0)),
            weight((D, D)), weight((D, D)),
            row(D), row(D),
            weight((D, D_FF)), weight((D_FF, D)),
            row(D), row(D),
        ],
        out_specs=pl.BlockSpec((1, T, D), lambda b, s: (b, s, 0)),
        out_shape=jax.ShapeDtypeStruct((B, S, D), F32),
        compiler_params=pltpu.CompilerParams(dimension_semantics=("arbitrary", "arbitrary"),
                                             vmem_limit_bytes=VMEM_LIMIT_BYTES),
        name="attn_mlp",
    )(x1, k, v, w_xq, w_xo, ln2_g, ln2_b, w_up, w_down, ln3_g, ln3_b)


def _position_tables(seq):
    half = RET_HEAD_DIM // 2
    pos = jnp.arange(seq, dtype=F32)
    inv = ROPE_BASE ** (-jnp.linspace(0.0, 1.0, half, dtype=F32))
    ang = pos[:, None] * inv[None, :]
    cos, sin = jnp.cos(ang), jnp.sin(ang)
    cos2 = jnp.concatenate([cos, cos], axis=-1)
    sin2 = jnp.concatenate([-sin, sin], axis=-1)
    qs = RET_HEAD_DIM ** -0.5
    C = RET_CHUNK
    log_g = jnp.log(1.0 - 2.0 ** (-5.0 - jnp.arange(RET_HEADS, dtype=F32)))
    idx = jnp.arange(C, dtype=F32)
    rel = idx[:, None] - idx[None, :]
    decay = jnp.where(rel >= 0, jnp.exp(log_g[:, None, None] * jnp.maximum(rel, 0.0)), 0.0)
    q_dec = jnp.exp(log_g[:, None] * (idx + 1.0))
    k_dec = jnp.exp(log_g[:, None] * (C - 1.0 - idx))
    chunk_dec = jnp.exp(log_g * C)
    bcast = lambda t: jnp.broadcast_to(t, (RET_HEADS, C, RET_HEAD_DIM))
    return (cos2 * qs, sin2 * qs, cos2, sin2, decay,
            bcast(q_dec[:, :, None]), bcast(k_dec[:, :, None]), bcast(chunk_dec[:, None, None]))


def kernel(x, mem, w_in, conv_w, conv_b, conv_ln_g, conv_ln_b, ret_gn_g, ret_gn_b, w_out,
           ln1_g, ln1_b, w_xq, w_xk, w_xv, w_xo, ln2_g, ln2_b, w_up, w_down, ln3_g, ln3_b):
    assert w_in.shape[0] == DEPTH == 1
    tabs = _position_tables(x.shape[1])
    bf = lambda w: w[0].astype(BF16)
    k, v = _kv_proj(mem, bf(w_xk), bf(w_xv))
    x1 = _mixer(x, bf(w_in), conv_w[0], conv_b, conv_ln_g, conv_ln_b, ret_gn_g, ret_gn_b, bf(w_out),
                ln1_g, ln1_b, tabs)
    return _attn_mlp(x1, k, v, bf(w_xq), bf(w_xo), ln2_g, ln2_b, bf(w_up), bf(w_down), ln3_g, ln3_b)
```

```python
import functools

import jax
import jax.numpy as jnp
from jax import lax
from jax.experimental import pallas as pl
from jax.experimental.pallas import tpu as pltpu

D_MODEL = 1024
N_MEM = 256
CONV_WIDTH = 512
CONV_KERNEL = 31
RET_WIDTH = 512
RET_HEADS = 4
RET_HEAD_DIM = 128
RET_CHUNK = 128
ROPE_BASE = 10000.0
IN_COLS = 2 * CONV_WIDTH + 4 * RET_WIDTH
XATTN_HEADS = 4
XATTN_HEAD_DIM = 256
D_FF = 4 * D_MODEL
LN_EPS = 1e-5
DEPTH = 1
DN_ALPHA = (2.0 * DEPTH) ** 0.25

CONV_HALO = 32
SUBLANES, LANES = 8, 128
CONV_ROWS = 128
MIXER_TILE = 512
ATTN_TILE = 512
FF_CHUNK = 1024
VMEM_LIMIT_BYTES = 56 * 1024 * 1024

BF16 = jnp.bfloat16
F32 = jnp.float32


def _dot(a, b):
    return jnp.dot(a, b, preferred_element_type=F32)


def _dot_nt(a, b):
    return lax.dot_general(a, b, (((1,), (1,)), ((), ())), preferred_element_type=F32)


def _sigmoid(x):
    return 1.0 / (1.0 + jnp.exp(-x))


def _layer_norm(x, g, b):
    mu = jnp.mean(x, axis=-1, keepdims=True)
    xc = x - mu
    var = jnp.mean(xc * xc, axis=-1, keepdims=True)
    return xc * lax.rsqrt(var + LN_EPS) * g + b


def _kv_kernel(mem_ref, wk_ref, wv_ref, k_ref, v_ref):
    m = mem_ref[0].astype(BF16)
    k_ref[0] = _dot(m, wk_ref[...]).astype(BF16)
    v_ref[0] = _dot(m, wv_ref[...]).astype(BF16)


def _kv_proj(mem, w_xk, w_xv):
    B = mem.shape[0]
    const = lambda b: (0, 0)
    return pl.pallas_call(
        _kv_kernel,
        grid=(B,),
        in_specs=[
            pl.BlockSpec((1, N_MEM, D_MODEL), lambda b: (b, 0, 0)),
            pl.BlockSpec((D_MODEL, D_MODEL), const),
            pl.BlockSpec((D_MODEL, D_MODEL), const),
        ],
        out_specs=[
            pl.BlockSpec((1, N_MEM, D_MODEL), lambda b: (b, 0, 0)),
            pl.BlockSpec((1, N_MEM, D_MODEL), lambda b: (b, 0, 0)),
        ],
        out_shape=[jax.ShapeDtypeStruct((B, N_MEM, D_MODEL), BF16)] * 2,
        compiler_params=pltpu.CompilerParams(dimension_semantics=("arbitrary",),
                                             vmem_limit_bytes=VMEM_LIMIT_BYTES),
        name="kv_proj",
    )(mem, w_xk, w_xv)


def _mixer_kernel(x_ref, w_in_ref, conv_w_ref, conv_b_ref, cln_g_ref, cln_b_ref, gn_g_ref, gn_b_ref,
                  w_out_ref, ln1_g_ref, ln1_b_ref, cosq_ref, sinq_ref, cosk_ref, sink_ref,
                  decay_ref, qdec_ref, kdec_ref, cdec_ref, o_ref, ubuf_ref, cbuf_ref, state_ref):
    T = MIXER_TILE
    s_idx = pl.program_id(1)

    @pl.when(s_idx == 0)
    def _():
        ubuf_ref[:, 0:2 * CONV_HALO, :] = jnp.zeros((ubuf_ref.shape[0], 2 * CONV_HALO, LANES), F32)
        state_ref[...] = jnp.zeros(state_ref.shape, F32)

    x = x_ref[0]
    xb = x.astype(BF16)

    def proj(lo, width):
        return _dot(xb, w_in_ref[:, lo:lo + width])

    a = proj(0, CONV_WIDTH)
    bgate = proj(CONV_WIDTH, CONV_WIDTH)
    u = a * _sigmoid(bgate)
    def strip_rows(c, t_start, n):
        return c // 2, pl.ds(2 * t_start + c % 2, n, stride=2), slice(None)

    n_strips = CONV_WIDTH // LANES
    for c in range(n_strips):
        ubuf_ref[strip_rows(c, CONV_HALO, T)] = u[:, c * LANES:(c + 1) * LANES]
    base = CONV_HALO - (CONV_KERNEL - 1)
    for c in range(n_strips):
        cols = slice(c * LANES, (c + 1) * LANES)
        for t0 in range(0, T, CONV_ROWS):
            acc = jnp.broadcast_to(conv_b_ref[:, cols], (CONV_ROWS, LANES))
            for j in range(CONV_KERNEL):
                acc = acc + ubuf_ref[strip_rows(c, t0 + base + j, CONV_ROWS)] * conv_w_ref[j:j + 1, cols]
            cbuf_ref[t0:t0 + CONV_ROWS, cols] = acc
    ubuf_ref[:, 0:2 * CONV_HALO, :] = ubuf_ref[:, 2 * T:2 * (T + CONV_HALO), :]
    cn = _layer_norm(cbuf_ref[...], cln_g_ref[...], cln_b_ref[...])
    conv_out = (cn * _sigmoid(cn)).astype(BF16)

    q_all = proj(2 * CONV_WIDTH, RET_WIDTH)
    k_all = proj(2 * CONV_WIDTH + RET_WIDTH, RET_WIDTH)
    v_all = proj(2 * CONV_WIDTH + 2 * RET_WIDTH, RET_WIDTH)
    g_all = proj(2 * CONV_WIDTH + 3 * RET_WIDTH, RET_WIDTH)
    cosq, sinq = cosq_ref[...], sinq_ref[...]
    cosk, sink = cosk_ref[...], sink_ref[...]
    half = RET_HEAD_DIM // 2
    C = RET_CHUNK
    ret_heads = []
    for h in range(RET_HEADS):
        lo = h * RET_HEAD_DIM
        qh = q_all[:, lo:lo + RET_HEAD_DIM]
        kh = k_all[:, lo:lo + RET_HEAD_DIM]
        vh = v_all[:, lo:lo + RET_HEAD_DIM].astype(BF16)
        qh = qh * cosq + pltpu.roll(qh, half, 1) * sinq
        kh = kh * cosk + pltpu.roll(kh, half, 1) * sink
        state = state_ref[h]
        decay = decay_ref[h]
        qdec = qdec_ref[h]
        kdec = kdec_ref[h]
        cdec = cdec_ref[h]
        ys = []
        for c in range(T // C):
            qc = qh[c * C:(c + 1) * C]
            kc = kh[c * C:(c + 1) * C]
            vc = vh[c * C:(c + 1) * C]
            scores = _dot_nt(qc.astype(BF16), kc.astype(BF16)) * decay
            lhs = jnp.concatenate([scores.astype(BF16), (qc * qdec).astype(BF16)], axis=1)
            rhs = jnp.concatenate([vc, state.astype(BF16)], axis=0)
            ys.append(_dot(lhs, rhs))
            kd_t = (kc * kdec).T.astype(BF16)
            state = state * cdec + _dot(kd_t, vc)
        state_ref[h] = state
        y = jnp.concatenate(ys, axis=0)
        mu = jnp.mean(y, axis=-1, keepdims=True)
        yc = y - mu
        var = jnp.mean(yc * yc, axis=-1, keepdims=True)
        yn = yc * lax.rsqrt(var + LN_EPS)
        yn = yn * gn_g_ref[:, lo:lo + RET_HEAD_DIM] + gn_b_ref[:, lo:lo + RET_HEAD_DIM]
        gh = g_all[:, lo:lo + RET_HEAD_DIM]
        ret_heads.append((gh * _sigmoid(gh) * yn).astype(BF16))

    merged = jnp.concatenate([conv_out] + ret_heads, axis=1)
    mix = _dot(merged, w_out_ref[...])
    o_ref[0] = _layer_norm(DN_ALPHA * x + mix, ln1_g_ref[...], ln1_b_ref[...])


def _mixer(x, w_in, conv_w, conv_b, cln_g, cln_b, gn_g, gn_b, w_out, ln1_g, ln1_b, tabs):
    B, S, D = x.shape
    T = MIXER_TILE
    const2 = lambda b, s: (0, 0)
    const3 = lambda b, s: (0, 0, 0)
    row = lambda n: pl.BlockSpec((1, n), const2)
    pos = lambda: pl.BlockSpec((T, RET_HEAD_DIM), lambda b, s: (s, 0))
    tab = lambda: pl.BlockSpec((RET_HEADS, RET_CHUNK, RET_HEAD_DIM), const3)
    return pl.pallas_call(
        _mixer_kernel,
        grid=(B, S // T),
        in_specs=[
            pl.BlockSpec((1, T, D), lambda b, s: (b, s, 0)),
            pl.BlockSpec((D, IN_COLS), const2),
            pl.BlockSpec((CONV_KERNEL, CONV_WIDTH), const2),
            row(CONV_WIDTH), row(CONV_WIDTH), row(CONV_WIDTH), row(RET_WIDTH), row(RET_WIDTH),
            pl.BlockSpec((D, D), const2),
            row(D), row(D),
            pos(), pos(), pos(), pos(),
            tab(), tab(), tab(), tab(),
        ],
        out_specs=pl.BlockSpec((1, T, D), lambda b, s: (b, s, 0)),
        out_shape=jax.ShapeDtypeStruct((B, S, D), F32),
        scratch_shapes=[
            pltpu.VMEM((CONV_WIDTH // (2 * LANES), 2 * (CONV_HALO + T), LANES), F32),
            pltpu.VMEM((T, CONV_WIDTH), F32),
            pltpu.VMEM((RET_HEADS, RET_HEAD_DIM, RET_HEAD_DIM), F32),
        ],
        compiler_params=pltpu.CompilerParams(dimension_semantics=("arbitrary", "arbitrary"),
                                             vmem_limit_bytes=VMEM_LIMIT_BYTES),
        name="mixer",
    )(x, w_in, conv_w, conv_b, cln_g, cln_b, gn_g, gn_b, w_out, ln1_g, ln1_b, *tabs)


def _attn_mlp_kernel(x_ref, k_ref, v_ref, wq_ref, wo_ref, ln2_g_ref, ln2_b_ref, wup_ref, wdn_ref,
                     ln3_g_ref, ln3_b_ref, o_ref):
    x = x_ref[0]
    q = (_dot(x.astype(BF16), wq_ref[...]) * (XATTN_HEAD_DIM ** -0.5)).astype(BF16)
    heads = []
    for h in range(XATTN_HEADS):
        lo = h * XATTN_HEAD_DIM
        s = _dot_nt(q[:, lo:lo + XATTN_HEAD_DIM], k_ref[0, :, lo:lo + XATTN_HEAD_DIM])
        p = jnp.exp(s - jnp.max(s, axis=-1, keepdims=True))
        l = jnp.sum(p, axis=-1, keepdims=True)
        o = _dot(p.astype(BF16), v_ref[0, :, lo:lo + XATTN_HEAD_DIM])
        heads.append((o * (1.0 / l)).astype(BF16))
    xa = _dot(jnp.concatenate(heads, axis=1), wo_ref[...])
    x2 = _layer_norm(DN_ALPHA * x + xa, ln2_g_ref[...], ln2_b_ref[...])
    x2b = x2.astype(BF16)
    ff = None
    for j in range(D_FF // FF_CHUNK):
        lo = j * FF_CHUNK
        hmid = jnp.maximum(_dot(x2b, wup_ref[:, lo:lo + FF_CHUNK]), 0.0)
        part = _dot((hmid * hmid).astype(BF16), wdn_ref[lo:lo + FF_CHUNK, :])
        ff = part if ff is None else ff + part
    o_ref[0] = _layer_norm(DN_ALPHA * x2 + ff, ln3_g_ref[...], ln3_b_ref[...])


def _attn_mlp(x1, k, v, w_xq, w_xo, ln2_g, ln2_b, w_up, w_down, ln3_g, ln3_b):
    B, S, D = x1.shape
    T = ATTN_TILE
    const2 = lambda b, s: (0, 0)
    row = lambda n: pl.BlockSpec((1, n), const2)
    weight = lambda shape: pl.BlockSpec(shape, const2, pipeline_mode=pl.Buffered(1))
    return pl.pallas_call(
        _attn_mlp_kernel,
        grid=(B, S // T),
        in_specs=[
            pl.BlockSpec((1, T, D), lambda b, s: (b, s, 0)),
            pl.BlockSpec((1, N_MEM, D), lambda b, s: (b, 0, 0)),
            pl.BlockSpec((1, N_MEM, D), lambda b, s: (b, 0, 0)),
            weight((D, D)), weight((D, D)),
            row(D), row(D),
            weight((D, D_FF)), weight((D_FF, D)),
            row(D), row(D),
        ],
        out_specs=pl.BlockSpec((1, T, D), lambda b, s: (b, s, 0)),
        out_shape=jax.ShapeDtypeStruct((B, S, D), F32),
        compiler_params=pltpu.CompilerParams(dimension_semantics=("arbitrary", "arbitrary"),
                                             vmem_limit_bytes=VMEM_LIMIT_BYTES),
        name="attn_mlp",
    )(x1, k, v, w_xq, w_xo, ln2_g, ln2_b, w_up, w_down, ln3_g, ln3_b)


def _position_tables(seq):
    half = RET_HEAD_DIM // 2
    pos = jnp.arange(seq, dtype=F32)
    inv = ROPE_BASE ** (-jnp.linspace(0.0, 1.0, half, dtype=F32))
    ang = pos[:, None] * inv[None, :]
    cos, sin = jnp.cos(ang), jnp.sin(ang)
    cos2 = jnp.concatenate([cos, cos], axis=-1)
    sin2 = jnp.concatenate([-sin, sin], axis=-1)
    qs = RET_HEAD_DIM ** -0.5
    C = RET_CHUNK
    log_g = jnp.log(1.0 - 2.0 ** (-5.0 - jnp.arange(RET_HEADS, dtype=F32)))
    idx = jnp.arange(C, dtype=F32)
    rel = idx[:, None] - idx[None, :]
    decay = jnp.where(rel >= 0, jnp.exp(log_g[:, None, None] * jnp.maximum(rel, 0.0)), 0.0)
    q_dec = jnp.exp(log_g[:, None] * (idx + 1.0))
    k_dec = jnp.exp(log_g[:, None] * (C - 1.0 - idx))
    chunk_dec = jnp.exp(log_g * C)
    bcast = lambda t: jnp.broadcast_to(t, (RET_HEADS, C, RET_HEAD_DIM))
    return (cos2 * qs, sin2 * qs, cos2, sin2, decay,
            bcast(q_dec[:, :, None]), bcast(k_dec[:, :, None]), bcast(chunk_dec[:, None, None]))


def kernel(x, mem, w_in, conv_w, conv_b, conv_ln_g, conv_ln_b, ret_gn_g, ret_gn_b, w_out,
           ln1_g, ln1_b, w_xq, w_xk, w_xv, w_xo, ln2_g, ln2_b, w_up, w_down, ln3_g, ln3_b):
    assert w_in.shape[0] == DEPTH == 1
    tabs = _position_tables(x.shape[1])
    bf = lambda w: w[0].astype(BF16)
    k, v = _kv_proj(mem, bf(w_xk), bf(w_xv))
    x1 = _mixer(x, bf(w_in), conv_w[0], conv_b, conv_ln_g, conv_ln_b, ret_gn_g, ret_gn_b, bf(w_out),
                ln1_g, ln1_b, tabs)
    return _attn_mlp(x1, k, v, bf(w_xq), bf(w_xo), ln2_g, ln2_b, bf(w_up), bf(w_down), ln3_g, ln3_b)
```

```python
import functools

import jax
import jax.numpy as jnp
from jax import lax
from jax.experimental import pallas as pl
from jax.experimental.pallas import tpu as pltpu

D_MODEL = 1024
N_MEM = 256
CONV_WIDTH = 512
CONV_KERNEL = 31
RET_WIDTH = 512
RET_HEADS = 4
RET_HEAD_DIM = 128
RET_CHUNK = 128
ROPE_BASE = 10000.0
IN_COLS = 2 * CONV_WIDTH + 4 * RET_WIDTH
XATTN_HEADS = 4
XATTN_HEAD_DIM = 256
D_FF = 4 * D_MODEL
LN_EPS = 1e-5
DEPTH = 1
DN_ALPHA = (2.0 * DEPTH) ** 0.25

CONV_HALO = 32
SUBLANES, LANES = 8, 128
CONV_ROWS = 128
N_STRIPS = CONV_WIDTH // LANES
PROJ_COLS = 512
N_PIECES = IN_COLS // PROJ_COLS
P_A, P_B, P_Q, P_K, P_V, P_G = range(N_PIECES)
MIXER_TILE = 512
ATTN_TILE = 512
FF_CHUNK = 1024
VMEM_LIMIT_BYTES = 56 * 1024 * 1024

BF16 = jnp.bfloat16
F32 = jnp.float32


def _dot(a, b):
    return jnp.dot(a, b, preferred_element_type=F32)


def _dot_nt(a, b):
    return lax.dot_general(a, b, (((1,), (1,)), ((), ())), preferred_element_type=F32)


def _sigmoid(x):
    return 1.0 / (1.0 + jnp.exp(-x))


def _layer_norm(x, g, b):
    mu = jnp.mean(x, axis=-1, keepdims=True)
    xc = x - mu
    var = jnp.mean(xc * xc, axis=-1, keepdims=True)
    return xc * lax.rsqrt(var + LN_EPS) * g + b


def _kv_kernel(mem_ref, wk_ref, wv_ref, k_ref, v_ref):
    m = mem_ref[0].astype(BF16)
    k_ref[0] = _dot(m, wk_ref[...]).astype(BF16)
    v_ref[0] = _dot(m, wv_ref[...]).astype(BF16)


def _kv_proj(mem, w_xk, w_xv):
    B = mem.shape[0]
    const = lambda b: (0, 0)
    return pl.pallas_call(
        _kv_kernel,
        grid=(B,),
        in_specs=[
            pl.BlockSpec((1, N_MEM, D_MODEL), lambda b: (b, 0, 0)),
            pl.BlockSpec((D_MODEL, D_MODEL), const),
            pl.BlockSpec((D_MODEL, D_MODEL), const),
        ],
        out_specs=[
            pl.BlockSpec((1, N_MEM, D_MODEL), lambda b: (b, 0, 0)),
            pl.BlockSpec((1, N_MEM, D_MODEL), lambda b: (b, 0, 0)),
        ],
        out_shape=[jax.ShapeDtypeStruct((B, N_MEM, D_MODEL), BF16)] * 2,
        compiler_params=pltpu.CompilerParams(dimension_semantics=("arbitrary",),
                                             vmem_limit_bytes=VMEM_LIMIT_BYTES),
        name="kv_proj",
    )(mem, w_xk, w_xv)


def _mixer_step(h_next_ref, h_ref, x_next_ref, x_ref, w_in_ref, conv_w_ref, conv_b_ref, cln_g_ref, cln_b_ref,
                gn_g_ref, gn_b_ref, w_out_ref, ln1_g_ref, ln1_b_ref, cosq_ref, sinq_ref, cosk_ref, sink_ref,
                decay_ref, qdec_ref, kdec_ref, cdec_ref, o_ref, xb_ref, ubuf_ref, cbuf_ref, state_ref):
    T = MIXER_TILE
    xb_ref[...] = x_next_ref[0].astype(BF16)

    def project(p):
        h_next_ref[p] = _dot(xb_ref[...], w_in_ref[p])

    u = h_ref[P_A] * _sigmoid(h_ref[P_B])
    def strip_rows(slab_parity, t_start, n):
        slab, parity = slab_parity
        return slab, pl.ds(2 * t_start + parity, n, stride=2), slice(None)

    for c in range(N_STRIPS):
        ubuf_ref[strip_rows(divmod(c, 2), CONV_HALO, T)] = u[:, c * LANES:(c + 1) * LANES]

    base = CONV_HALO - (CONV_KERNEL - 1)

    def conv_strip(c):
        where = divmod(c, 2)
        for t0 in range(0, T, CONV_ROWS):
            acc = jnp.broadcast_to(conv_b_ref[c], (CONV_ROWS, LANES))
            for j in range(CONV_KERNEL):
                acc = acc + ubuf_ref[strip_rows(where, t0 + base + j, CONV_ROWS)] * conv_w_ref[c, j:j + 1, :]
            cbuf_ref[c, t0:t0 + CONV_ROWS, :] = acc
        project(c)

    for c in range(N_STRIPS):
        pl.when(pl.program_id(0) >= 0)(functools.partial(conv_strip, c))
    ubuf_ref[:, 0:2 * CONV_HALO, :] = ubuf_ref[:, 2 * T:2 * (T + CONV_HALO), :]
    conv = jnp.concatenate([cbuf_ref[c] for c in range(N_STRIPS)], axis=1)
    cn = _layer_norm(conv, cln_g_ref[...], cln_b_ref[...])
    conv_out = (cn * _sigmoid(cn)).astype(BF16)
    for p in range(N_STRIPS, N_PIECES):
        project(p)

    cosq, sinq = cosq_ref[...], sinq_ref[...]
    cosk, sink = cosk_ref[...], sink_ref[...]
    half = RET_HEAD_DIM // 2
    C = RET_CHUNK
    n_chunks = T // C
    head_cols = [slice(h * RET_HEAD_DIM, (h + 1) * RET_HEAD_DIM) for h in range(RET_HEADS)]
    q, v, scores, kv = [], [], [], []
    for h, hs in enumerate(head_cols):
        qh = h_ref[P_Q, :, hs]
        kh = h_ref[P_K, :, hs]
        qh = qh * cosq + pltpu.roll(qh, half, 1) * sinq
        kh = kh * cosk + pltpu.roll(kh, half, 1) * sink
        vh = h_ref[P_V, :, hs].astype(BF16)
        q.append(qh)
        v.append(vh)
        decay, kdec = decay_ref[h], kdec_ref[h]
        scores_h, kv_h = [], []
        for c in range(n_chunks):
            rows = slice(c * C, (c + 1) * C)
            scores_h.append((_dot_nt(qh[rows].astype(BF16), kh[rows].astype(BF16)) * decay).astype(BF16))
            kv_h.append(_dot((kh[rows] * kdec).T.astype(BF16), vh[rows]))
        scores.append(scores_h)
        kv.append(kv_h)
    ret_heads = []
    for h, hs in enumerate(head_cols):
        qdec, cdec = qdec_ref[h], cdec_ref[h]
        state = state_ref[h]
        ys = []
        for c in range(n_chunks):
            rows = slice(c * C, (c + 1) * C)
            lhs = jnp.concatenate([scores[h][c], (q[h][rows] * qdec).astype(BF16)], axis=1)
            rhs = jnp.concatenate([v[h][rows], state.astype(BF16)], axis=0)
            ys.append(_dot(lhs, rhs))
            state = state * cdec + kv[h][c]
        state_ref[h] = state
        y = jnp.concatenate(ys, axis=0)
        mu = jnp.mean(y, axis=-1, keepdims=True)
        yc = y - mu
        var = jnp.mean(yc * yc, axis=-1, keepdims=True)
        yn = yc * lax.rsqrt(var + LN_EPS)
        yn = yn * gn_g_ref[:, hs] + gn_b_ref[:, hs]
        gh = h_ref[P_G, :, hs]
        ret_heads.append((gh * _sigmoid(gh) * yn).astype(BF16))

    merged = jnp.concatenate([conv_out] + ret_heads, axis=1)
    mix = _dot(merged, w_out_ref[...])
    o_ref[0] = _layer_norm(DN_ALPHA * x_ref[0] + mix, ln1_g_ref[...], ln1_b_ref[...])


def _mixer_kernel(tiles_per_seq, x_next_ref, x_ref, *refs):
    *io_refs, h_even_ref, h_odd_ref, xb_ref, ubuf_ref, cbuf_ref, state_ref = refs
    n = pl.program_id(0)
    tile = jnp.maximum(n - 1, 0)

    @pl.when(n == 0)
    def _():
        h_odd_ref[...] = jnp.zeros(h_odd_ref.shape, F32)

    @pl.when(lax.rem(tile, tiles_per_seq) == 0)
    def _():
        ubuf_ref[:, 0:2 * CONV_HALO, :] = jnp.zeros((ubuf_ref.shape[0], 2 * CONV_HALO, LANES), F32)
        state_ref[...] = jnp.zeros(state_ref.shape, F32)

    def step(h_next_ref, h_ref):
        _mixer_step(h_next_ref, h_ref, x_next_ref, x_ref, *io_refs, xb_ref, ubuf_ref, cbuf_ref, state_ref)

    @pl.when(lax.rem(n, 2) == 0)
    def _():
        step(h_even_ref, h_odd_ref)

    @pl.when(lax.rem(n, 2) == 1)
    def _():
        step(h_odd_ref, h_even_ref)


def _mixer(x, w_in, conv_w, conv_b, cln_g, cln_b, gn_g, gn_b, w_out, ln1_g, ln1_b, tabs):
    B, S, D = x.shape
    T = MIXER_TILE
    tiles_per_seq = S // T
    n_tiles = B * tiles_per_seq

    def tile_index(tile):
        return lax.div(tile, tiles_per_seq), lax.rem(tile, tiles_per_seq)

    def next_tile(n):
        b, s = tile_index(jnp.minimum(n, n_tiles - 1))
        return b, s, 0

    def this_tile(n):
        b, s = tile_index(jnp.maximum(n - 1, 0))
        return b, s, 0

    const2 = lambda n: (0, 0)
    const3 = lambda n: (0, 0, 0)
    row = lambda width: pl.BlockSpec((1, width), const2)
    pos = lambda: pl.BlockSpec((T, RET_HEAD_DIM), lambda n: (this_tile(n)[1], 0))
    tab = lambda: pl.BlockSpec((RET_HEADS, RET_CHUNK, RET_HEAD_DIM), const3)
    w_in_p = w_in.reshape(D, N_PIECES, PROJ_COLS).transpose(1, 0, 2)
    conv_w_s = conv_w.reshape(CONV_KERNEL, N_STRIPS, LANES).transpose(1, 0, 2)
    conv_b_s = conv_b.reshape(N_STRIPS, 1, LANES)
    return pl.pallas_call(
        functools.partial(_mixer_kernel, tiles_per_seq),
        grid=(n_tiles + 1,),
        in_specs=[
            pl.BlockSpec((1, T, D), next_tile),
            pl.BlockSpec((1, T, D), this_tile),
            pl.BlockSpec((N_PIECES, D, PROJ_COLS), const3, pipeline_mode=pl.Buffered(1)),
            pl.BlockSpec((N_STRIPS, CONV_KERNEL, LANES), const3),
            pl.BlockSpec((N_STRIPS, 1, LANES), const3),
            row(CONV_WIDTH), row(CONV_WIDTH), row(RET_WIDTH), row(RET_WIDTH),
            pl.BlockSpec((D, D), const2, pipeline_mode=pl.Buffered(1)),
            row(D), row(D),
            pos(), pos(), pos(), pos(),
            tab(), tab(), tab(), tab(),
        ],
        out_specs=pl.BlockSpec((1, T, D), this_tile),
        out_shape=jax.ShapeDtypeStruct((B, S, D), F32),
        scratch_shapes=[
            pltpu.VMEM((N_PIECES, T, PROJ_COLS), F32),
            pltpu.VMEM((N_PIECES, T, PROJ_COLS), F32),
            pltpu.VMEM((T, D), BF16),
            pltpu.VMEM((N_STRIPS // 2, 2 * (CONV_HALO + T), LANES), F32),
            pltpu.VMEM((N_STRIPS, T, LANES), F32),
            pltpu.VMEM((RET_HEADS, RET_HEAD_DIM, RET_HEAD_DIM), F32),
        ],
        compiler_params=pltpu.CompilerParams(dimension_semantics=("arbitrary",),
                                             vmem_limit_bytes=VMEM_LIMIT_BYTES),
        name="mixer",
    )(x, x, w_in_p, conv_w_s, conv_b_s, cln_g, cln_b, gn_g, gn_b, w_out, ln1_g, ln1_b, *tabs)


def _attn_mlp_kernel(x_ref, k_ref, v_ref, wq_ref, wo_ref, ln2_g_ref, ln2_b_ref, wup_ref, wdn_ref,
                     ln3_g_ref, ln3_b_ref, o_ref):
    x = x_ref[0]
    q = (_dot(x.astype(BF16), wq_ref[...]) * (XATTN_HEAD_DIM ** -0.5)).astype(BF16)
    heads = []
    for h in range(XATTN_HEADS):
        lo = h * XATTN_HEAD_DIM
        s = _dot_nt(q[:, lo:lo + XATTN_HEAD_DIM], k_ref[0, :, lo:lo + XATTN_HEAD_DIM])
        p = jnp.exp(s - jnp.max(s, axis=-1, keepdims=True))
        l = jnp.sum(p, axis=-1, keepdims=True)
        o = _dot(p.astype(BF16), v_ref[0, :, lo:lo + XATTN_HEAD_DIM])
        heads.append((o * (1.0 / l)).astype(BF16))
    xa = _dot(jnp.concatenate(heads, axis=1), wo_ref[...])
    x2 = _layer_norm(DN_ALPHA * x + xa, ln2_g_ref[...], ln2_b_ref[...])
    x2b = x2.astype(BF16)
    ff = None
    for j in range(D_FF // FF_CHUNK):
        lo = j * FF_CHUNK
        hmid = jnp.maximum(_dot(x2b, wup_ref[:, lo:lo + FF_CHUNK]), 0.0)
        part = _dot((hmid * hmid).astype(BF16), wdn_ref[lo:lo + FF_CHUNK, :])
        ff = part if ff is None else ff + part
    o_ref[0] = _layer_norm(DN_ALPHA * x2 + ff, ln3_g_ref[...], ln3_b_ref[...])


def _attn_mlp(x1, k, v, w_xq, w_xo, ln2_g, ln2_b, w_up, w_down, ln3_g, ln3_b):
    B, S, D = x1.shape
    T = ATTN_TILE
    const2 = lambda b, s: (0, 0)
    row = lambda n: pl.BlockSpec((1, n), const2)
    weight = lambda shape: pl.BlockSpec(shape, const2, pipeline_mode=pl.Buffered(1))
    return pl.pallas_call(
        _attn_mlp_kernel,
        grid=(B, S // T),
        in_specs=[
            pl.BlockSpec((1, T, D), lambda b, s: (b, s, 0)),
            pl.BlockSpec((1, N_MEM, D), lambda b, s: (b, 0, 0)),
            pl.BlockSpec((1, N_MEM, D), lambda b, s: (b, 0, 0)),
            weight((D, D)), weight((D, D)),
            row(D), row(D),
            weight((D, D_FF)), weight((D_FF, D)),
            row(D), row(D),
        ],
        out_specs=pl.BlockSpec((1, T, D), lambda b, s: (b, s, 0)),
        out_shape=jax.ShapeDtypeStruct((B, S, D), F32),
        compiler_params=pltpu.CompilerParams(dimension_semantics=("arbitrary", "arbitrary"),
                                             vmem_limit_bytes=VMEM_LIMIT_BYTES),
        name="attn_mlp",
    )(x1, k, v, w_xq, w_xo, ln2_g, ln2_b, w_up, w_down, ln3_g, ln3_b)


def _position_tables(seq):
    half = RET_HEAD_DIM // 2
    pos = jnp.arange(seq, dtype=F32)
    inv = ROPE_BASE ** (-jnp.linspace(0.0, 1.0, half, dtype=F32))
    ang = pos[:, None] * inv[None, :]
    cos, sin = jnp.cos(ang), jnp.sin(ang)
    cos2 = jnp.concatenate([cos, cos], axis=-1)
    sin2 = jnp.concatenate([-sin, sin], axis=-1)
    qs = RET_HEAD_DIM ** -0.5
    C = RET_CHUNK
    log_g = jnp.log(1.0 - 2.0 ** (-5.0 - jnp.arange(RET_HEADS, dtype=F32)))
    idx = jnp.arange(C, dtype=F32)
    rel = idx[:, None] - idx[None, :]
    decay = jnp.where(rel >= 0, jnp.exp(log_g[:, None, None] * jnp.maximum(rel, 0.0)), 0.0)
    q_dec = jnp.exp(log_g[:, None] * (idx + 1.0))
    k_dec = jnp.exp(log_g[:, None] * (C - 1.0 - idx))
    chunk_dec = jnp.exp(log_g * C)
    bcast = lambda t: jnp.broadcast_to(t, (RET_HEADS, C, RET_HEAD_DIM))
    return (cos2 * qs, sin2 * qs, cos2, sin2, decay,
            bcast(q_dec[:, :, None]), bcast(k_dec[:, :, None]), bcast(chunk_dec[:, None, None]))


def kernel(x, mem, w_in, conv_w, conv_b, conv_ln_g, conv_ln_b, ret_gn_g, ret_gn_b, w_out,
           ln1_g, ln1_b, w_xq, w_xk, w_xv, w_xo, ln2_g, ln2_b, w_up, w_down, ln3_g, ln3_b):
    assert w_in.shape[0] == DEPTH == 1
    assert RET_CHUNK == RET_HEAD_DIM
    tabs = _position_tables(x.shape[1])
    bf = lambda w: w[0].astype(BF16)
    k, v = _kv_proj(mem, bf(w_xk), bf(w_xv))
    x1 = _mixer(x, bf(w_in), conv_w[0], conv_b, conv_ln_g, conv_ln_b, ret_gn_g, ret_gn_b, bf(w_out),
                ln1_g, ln1_b, tabs)
    return _attn_mlp(x1, k, v, bf(w_xq), bf(w_xo), ln2_g, ln2_b, bf(w_up), bf(w_down), ln3_g, ln3_b)
```

```python
import functools

import jax
import jax.numpy as jnp
from jax import lax
from jax.experimental import pallas as pl
from jax.experimental.pallas import tpu as pltpu

D_MODEL = 1024
N_MEM = 256
CONV_WIDTH = 512
CONV_KERNEL = 31
RET_WIDTH = 512
RET_HEADS = 4
RET_HEAD_DIM = 128
RET_CHUNK = 128
ROPE_BASE = 10000.0
IN_COLS = 2 * CONV_WIDTH + 4 * RET_WIDTH
XATTN_HEADS = 4
XATTN_HEAD_DIM = 256
D_FF = 4 * D_MODEL
LN_EPS = 1e-5
DEPTH = 1
DN_ALPHA = (2.0 * DEPTH) ** 0.25

CONV_HALO = 32
SUBLANES, LANES = 8, 128
CONV_ROWS = 128
N_STRIPS = CONV_WIDTH // LANES
PROJ_COLS = 512
N_PIECES = IN_COLS // PROJ_COLS
P_A, P_B, P_Q, P_K, P_V, P_G = range(N_PIECES)
MIXER_TILE = 512
ATTN_TILE = 512
FF_CHUNK = 1024
VMEM_LIMIT_BYTES = 56 * 1024 * 1024

BF16 = jnp.bfloat16
F32 = jnp.float32


def _dot(a, b):
    return jnp.dot(a, b, preferred_element_type=F32)


def _dot_nt(a, b):
    return lax.dot_general(a, b, (((1,), (1,)), ((), ())), preferred_element_type=F32)


def _sigmoid(x):
    return 1.0 / (1.0 + jnp.exp(-x))


def _layer_norm(x, g, b):
    mu = jnp.mean(x, axis=-1, keepdims=True)
    xc = x - mu
    var = jnp.mean(xc * xc, axis=-1, keepdims=True)
    return xc * lax.rsqrt(var + LN_EPS) * g + b


def _kv_kernel(mem_ref, wk_ref, wv_ref, k_ref, v_ref):
    m = mem_ref[0].astype(BF16)
    k_ref[0] = _dot(m, wk_ref[...]).astype(BF16)
    v_ref[0] = _dot(m, wv_ref[...]).astype(BF16)


def _kv_proj(mem, w_xk, w_xv):
    B = mem.shape[0]
    const = lambda b: (0, 0)
    return pl.pallas_call(
        _kv_kernel,
        grid=(B,),
        in_specs=[
            pl.BlockSpec((1, N_MEM, D_MODEL), lambda b: (b, 0, 0)),
            pl.BlockSpec((D_MODEL, D_MODEL), const),
            pl.BlockSpec((D_MODEL, D_MODEL), const),
        ],
        out_specs=[
            pl.BlockSpec((1, N_MEM, D_MODEL), lambda b: (b, 0, 0)),
            pl.BlockSpec((1, N_MEM, D_MODEL), lambda b: (b, 0, 0)),
        ],
        out_shape=[jax.ShapeDtypeStruct((B, N_MEM, D_MODEL), BF16)] * 2,
        compiler_params=pltpu.CompilerParams(dimension_semantics=("arbitrary",),
                                             vmem_limit_bytes=VMEM_LIMIT_BYTES),
        name="kv_proj",
    )(mem, w_xk, w_xv)


def _mixer_step(h_next_ref, h_ref, x_keep_ref, x_ref, seq_tile, x_next_ref, w_in_ref, conv_w_ref, conv_b_ref,
                cln_g_ref, cln_b_ref, gn_g_ref, gn_b_ref, w_out_ref, ln1_g_ref, ln1_b_ref, cos_ref, sin_ref,
                decay_ref, qdec_ref, kdec_ref, cdec_ref, o_ref, xb_ref, ubuf_ref, cbuf_ref, state_ref):
    T = MIXER_TILE
    x_next = x_next_ref[0]
    x_keep_ref[...] = x_next
    xb_ref[...] = x_next.astype(BF16)

    def project(p):
        h_next_ref[p] = _dot(xb_ref[...], w_in_ref[p])

    u = h_ref[P_A] * _sigmoid(h_ref[P_B])
    def strip_rows(slab_parity, t_start, n):
        slab, parity = slab_parity
        return slab, pl.ds(2 * t_start + parity, n, stride=2), slice(None)

    for c in range(N_STRIPS):
        ubuf_ref[strip_rows(divmod(c, 2), CONV_HALO, T)] = u[:, c * LANES:(c + 1) * LANES]

    base = CONV_HALO - (CONV_KERNEL - 1)

    def conv_strip(c):
        where = divmod(c, 2)
        for t0 in range(0, T, CONV_ROWS):
            acc = jnp.broadcast_to(conv_b_ref[c], (CONV_ROWS, LANES))
            for j in range(CONV_KERNEL):
                acc = acc + ubuf_ref[strip_rows(where, t0 + base + j, CONV_ROWS)] * conv_w_ref[c, j:j + 1, :]
            cbuf_ref[c, t0:t0 + CONV_ROWS, :] = acc
        project(c)

    for c in range(N_STRIPS):
        pl.when(pl.program_id(0) >= 0)(functools.partial(conv_strip, c))
    ubuf_ref[:, 0:2 * CONV_HALO, :] = ubuf_ref[:, 2 * T:2 * (T + CONV_HALO), :]
    conv = jnp.concatenate([cbuf_ref[c] for c in range(N_STRIPS)], axis=1)
    cn = _layer_norm(conv, cln_g_ref[...], cln_b_ref[...])
    conv_out = (cn * _sigmoid(cn)).astype(BF16)
    for p in range(N_STRIPS, N_PIECES):
        project(p)

    pos_rows = pl.ds(pl.multiple_of(seq_tile * T, T), T)
    cos, sin = cos_ref[pos_rows, :], sin_ref[pos_rows, :]
    half = RET_HEAD_DIM // 2
    C = RET_CHUNK
    n_chunks = T // C
    head_cols = [slice(h * RET_HEAD_DIM, (h + 1) * RET_HEAD_DIM) for h in range(RET_HEADS)]
    q, v, scores, kv = [], [], [], []
    for h, hs in enumerate(head_cols):
        qh = h_ref[P_Q, :, hs]
        kh = h_ref[P_K, :, hs]
        qh = qh * cos + pltpu.roll(qh, half, 1) * sin
        kh = kh * cos + pltpu.roll(kh, half, 1) * sin
        vh = h_ref[P_V, :, hs].astype(BF16)
        q.append(qh)
        v.append(vh)
        decay, kdec = decay_ref[h], kdec_ref[h]
        scores_h, kv_h = [], []
        for c in range(n_chunks):
            rows = slice(c * C, (c + 1) * C)
            scores_h.append((_dot_nt(qh[rows].astype(BF16), kh[rows].astype(BF16)) * decay).astype(BF16))
            kv_h.append(_dot((kh[rows] * kdec).T.astype(BF16), vh[rows]))
        scores.append(scores_h)
        kv.append(kv_h)
    ret_heads = []
    for h, hs in enumerate(head_cols):
        qdec, cdec = qdec_ref[h], cdec_ref[h]
        state = state_ref[h]
        ys = []
        for c in range(n_chunks):
            rows = slice(c * C, (c + 1) * C)
            lhs = jnp.concatenate([scores[h][c], (q[h][rows] * qdec).astype(BF16)], axis=1)
            rhs = jnp.concatenate([v[h][rows], state.astype(BF16)], axis=0)
            ys.append(_dot(lhs, rhs))
            state = state * cdec + kv[h][c]
        state_ref[h] = state
        y = jnp.concatenate(ys, axis=0)
        mu = jnp.mean(y, axis=-1, keepdims=True)
        yc = y - mu
        var = jnp.mean(yc * yc, axis=-1, keepdims=True)
        yn = yc * lax.rsqrt(var + LN_EPS)
        yn = yn * gn_g_ref[:, hs] + gn_b_ref[:, hs]
        gh = h_ref[P_G, :, hs]
        ret_heads.append((gh * _sigmoid(gh) * yn).astype(BF16))

    merged = jnp.concatenate([conv_out] + ret_heads, axis=1)
    mix = _dot(merged, w_out_ref[...])
    o_ref[0] = _layer_norm(DN_ALPHA * x_ref[...] + mix, ln1_g_ref[...], ln1_b_ref[...])


def _mixer_kernel(tiles_per_seq, *refs):
    *io_refs, h_even_ref, h_odd_ref, x_even_ref, x_odd_ref, xb_ref, ubuf_ref, cbuf_ref, state_ref = refs
    n = pl.program_id(0)
    seq_tile = lax.rem(jnp.maximum(n - 1, 0), tiles_per_seq)

    @pl.when(n == 0)
    def _():
        h_odd_ref[...] = jnp.zeros(h_odd_ref.shape, F32)
        x_odd_ref[...] = jnp.zeros(x_odd_ref.shape, F32)

    @pl.when(seq_tile == 0)
    def _():
        ubuf_ref[:, 0:2 * CONV_HALO, :] = jnp.zeros((ubuf_ref.shape[0], 2 * CONV_HALO, LANES), F32)
        state_ref[...] = jnp.zeros(state_ref.shape, F32)

    def step(h_next_ref, h_ref, x_keep_ref, x_ref):
        _mixer_step(h_next_ref, h_ref, x_keep_ref, x_ref, seq_tile, *io_refs, xb_ref, ubuf_ref, cbuf_ref,
                    state_ref)

    @pl.when(lax.rem(n, 2) == 0)
    def _():
        step(h_even_ref, h_odd_ref, x_even_ref, x_odd_ref)

    @pl.when(lax.rem(n, 2) == 1)
    def _():
        step(h_odd_ref, h_even_ref, x_odd_ref, x_even_ref)


def _mixer(x, w_in, conv_w, conv_b, cln_g, cln_b, gn_g, gn_b, w_out, ln1_g, ln1_b, tabs):
    B, S, D = x.shape
    T = MIXER_TILE
    tiles_per_seq = S // T
    n_tiles = B * tiles_per_seq

    def tile_index(tile):
        return lax.div(tile, tiles_per_seq), lax.rem(tile, tiles_per_seq)

    def next_tile(n):
        b, s = tile_index(jnp.minimum(n, n_tiles - 1))
        return b, s, 0

    def this_tile(n):
        b, s = tile_index(jnp.maximum(n - 1, 0))
        return b, s, 0

    const2 = lambda n: (0, 0)
    const3 = lambda n: (0, 0, 0)
    row = lambda width: pl.BlockSpec((1, width), const2)
    pos = lambda: pl.BlockSpec((S, RET_HEAD_DIM), const2, pipeline_mode=pl.Buffered(1))
    tab = lambda: pl.BlockSpec((RET_HEADS, RET_CHUNK, RET_HEAD_DIM), const3)
    w_in_p = w_in.reshape(D, N_PIECES, PROJ_COLS).transpose(1, 0, 2)
    conv_w_s = conv_w.reshape(CONV_KERNEL, N_STRIPS, LANES).transpose(1, 0, 2)
    conv_b_s = conv_b.reshape(N_STRIPS, 1, LANES)
    return pl.pallas_call(
        functools.partial(_mixer_kernel, tiles_per_seq),
        grid=(n_tiles + 1,),
        in_specs=[
            pl.BlockSpec((1, T, D), next_tile),
            pl.BlockSpec((N_PIECES, D, PROJ_COLS), const3, pipeline_mode=pl.Buffered(1)),
            pl.BlockSpec((N_STRIPS, CONV_KERNEL, LANES), const3),
            pl.BlockSpec((N_STRIPS, 1, LANES), const3),
            row(CONV_WIDTH), row(CONV_WIDTH), row(RET_WIDTH), row(RET_WIDTH),
            pl.BlockSpec((D, D), const2, pipeline_mode=pl.Buffered(1)),
            row(D), row(D),
            pos(), pos(),
            tab(), tab(), tab(), tab(),
        ],
        out_specs=pl.BlockSpec((1, T, D), this_tile),
        out_shape=jax.ShapeDtypeStruct((B, S, D), F32),
        scratch_shapes=[
            pltpu.VMEM((N_PIECES, T, PROJ_COLS), F32),
            pltpu.VMEM((N_PIECES, T, PROJ_COLS), F32),
            pltpu.VMEM((T, D), F32),
            pltpu.VMEM((T, D), F32),
            pltpu.VMEM((T, D), BF16),
            pltpu.VMEM((N_STRIPS // 2, 2 * (CONV_HALO + T), LANES), F32),
            pltpu.VMEM((N_STRIPS, T, LANES), F32),
            pltpu.VMEM((RET_HEADS, RET_HEAD_DIM, RET_HEAD_DIM), F32),
        ],
        compiler_params=pltpu.CompilerParams(dimension_semantics=("arbitrary",),
                                             vmem_limit_bytes=VMEM_LIMIT_BYTES),
        name="mixer",
    )(x, w_in_p, conv_w_s, conv_b_s, cln_g, cln_b, gn_g, gn_b, w_out, ln1_g, ln1_b, *tabs)


def _attn_mlp_kernel(x_ref, k_ref, v_ref, wq_ref, wo_ref, ln2_g_ref, ln2_b_ref, wup_ref, wdn_ref,
                     ln3_g_ref, ln3_b_ref, o_ref):
    x = x_ref[0]
    q = (_dot(x.astype(BF16), wq_ref[...]) * (XATTN_HEAD_DIM ** -0.5)).astype(BF16)
    heads = []
    for h in range(XATTN_HEADS):
        lo = h * XATTN_HEAD_DIM
        s = _dot_nt(q[:, lo:lo + XATTN_HEAD_DIM], k_ref[0, :, lo:lo + XATTN_HEAD_DIM])
        p = jnp.exp(s - jnp.max(s, axis=-1, keepdims=True))
        l = jnp.sum(p, axis=-1, keepdims=True)
        o = _dot(p.astype(BF16), v_ref[0, :, lo:lo + XATTN_HEAD_DIM])
        heads.append((o * (1.0 / l)).astype(BF16))
    xa = _dot(jnp.concatenate(heads, axis=1), wo_ref[...])
    x2 = _layer_norm(DN_ALPHA * x + xa, ln2_g_ref[...], ln2_b_ref[...])
    x2b = x2.astype(BF16)
    ff = None
    for j in range(D_FF // FF_CHUNK):
        lo = j * FF_CHUNK
        hmid = jnp.maximum(_dot(x2b, wup_ref[:, lo:lo + FF_CHUNK]), 0.0)
        part = _dot((hmid * hmid).astype(BF16), wdn_ref[lo:lo + FF_CHUNK, :])
        ff = part if ff is None else ff + part
    o_ref[0] = _layer_norm(DN_ALPHA * x2 + ff, ln3_g_ref[...], ln3_b_ref[...])


def _attn_mlp(x1, k, v, w_xq, w_xo, ln2_g, ln2_b, w_up, w_down, ln3_g, ln3_b):
    B, S, D = x1.shape
    T = ATTN_TILE
    const2 = lambda b, s: (0, 0)
    row = lambda n: pl.BlockSpec((1, n), const2)
    weight = lambda shape: pl.BlockSpec(shape, const2, pipeline_mode=pl.Buffered(1))
    return pl.pallas_call(
        _attn_mlp_kernel,
        grid=(B, S // T),
        in_specs=[
            pl.BlockSpec((1, T, D), lambda b, s: (b, s, 0)),
            pl.BlockSpec((1, N_MEM, D), lambda b, s: (b, 0, 0)),
            pl.BlockSpec((1, N_MEM, D), lambda b, s: (b, 0, 0)),
            weight((D, D)), weight((D, D)),
            row(D), row(D),
            weight((D, D_FF)), weight((D_FF, D)),
            row(D), row(D),
        ],
        out_specs=pl.BlockSpec((1, T, D), lambda b, s: (b, s, 0)),
        out_shape=jax.ShapeDtypeStruct((B, S, D), F32),
        compiler_params=pltpu.CompilerParams(dimension_semantics=("arbitrary", "arbitrary"),
                                             vmem_limit_bytes=VMEM_LIMIT_BYTES),
        name="attn_mlp",
    )(x1, k, v, w_xq, w_xo, ln2_g, ln2_b, w_up, w_down, ln3_g, ln3_b)


def _position_tables(seq):
    half = RET_HEAD_DIM // 2
    pos = jnp.arange(seq, dtype=F32)
    inv = ROPE_BASE ** (-jnp.linspace(0.0, 1.0, half, dtype=F32))
    ang = pos[:, None] * inv[None, :]
    cos, sin = jnp.cos(ang), jnp.sin(ang)
    cos2 = jnp.concatenate([cos, cos], axis=-1)
    sin2 = jnp.concatenate([-sin, sin], axis=-1)
    qs = RET_HEAD_DIM ** -0.5
    C = RET_CHUNK
    log_g = jnp.log(1.0 - 2.0 ** (-5.0 - jnp.arange(RET_HEADS, dtype=F32)))
    idx = jnp.arange(C, dtype=F32)
    rel = idx[:, None] - idx[None, :]
    decay = jnp.where(rel >= 0, jnp.exp(log_g[:, None, None] * jnp.maximum(rel, 0.0)), 0.0)
    q_dec = jnp.exp(log_g[:, None] * (idx + 1.0))
    k_dec = jnp.exp(log_g[:, None] * (C - 1.0 - idx))
    chunk_dec = jnp.exp(log_g * C)
    bcast = lambda t: jnp.broadcast_to(t, (RET_HEADS, C, RET_HEAD_DIM))
    return (cos2, sin2, decay * qs,
            bcast(q_dec[:, :, None]) * qs, bcast(k_dec[:, :, None]), bcast(chunk_dec[:, None, None]))


def kernel(x, mem, w_in, conv_w, conv_b, conv_ln_g, conv_ln_b, ret_gn_g, ret_gn_b, w_out,
           ln1_g, ln1_b, w_xq, w_xk, w_xv, w_xo, ln2_g, ln2_b, w_up, w_down, ln3_g, ln3_b):
    assert w_in.shape[0] == DEPTH == 1
    assert RET_CHUNK == RET_HEAD_DIM
    tabs = _position_tables(x.shape[1])
    bf = lambda w: w[0].astype(BF16)
    k, v = _kv_proj(mem, bf(w_xk), bf(w_xv))
    x1 = _mixer(x, bf(w_in), conv_w[0], conv_b, conv_ln_g, conv_ln_b, ret_gn_g, ret_gn_b, bf(w_out),
                ln1_g, ln1_b, tabs)
    return _attn_mlp(x1, k, v, bf(w_xq), bf(w_xo), ln2_g, ln2_b, bf(w_up), bf(w_down), ln3_g, ln3_b)
```

```python
import functools

import jax
import jax.numpy as jnp
from jax import lax
from jax.experimental import pallas as pl
from jax.experimental.pallas import tpu as pltpu

D_MODEL = 1024
N_MEM = 256
CONV_WIDTH = 512
CONV_KERNEL = 31
RET_WIDTH = 512
RET_HEADS = 4
RET_HEAD_DIM = 128
RET_CHUNK = 128
ROPE_BASE = 10000.0
IN_COLS = 2 * CONV_WIDTH + 4 * RET_WIDTH
XATTN_HEADS = 4
XATTN_HEAD_DIM = 256
D_FF = 4 * D_MODEL
LN_EPS = 1e-5
DEPTH = 1
DN_ALPHA = (2.0 * DEPTH) ** 0.25

CONV_HALO = 32
SUBLANES, LANES = 8, 128
CONV_ROWS = 128
N_STRIPS = CONV_WIDTH // LANES
PROJ_COLS = 512
N_PIECES = IN_COLS // PROJ_COLS
P_A, P_B, P_Q, P_K, P_V, P_G = range(N_PIECES)
MIXER_TILE = 512
ATTN_TILE = 512
FF_CHUNK = 1024
VMEM_LIMIT_BYTES = 56 * 1024 * 1024

BF16 = jnp.bfloat16
F32 = jnp.float32


def _dot(a, b):
    return jnp.dot(a, b, preferred_element_type=F32)


def _dot_nt(a, b):
    return lax.dot_general(a, b, (((1,), (1,)), ((), ())), preferred_element_type=F32)


def _sigmoid(x):
    return 1.0 / (1.0 + jnp.exp(-x))


def _layer_norm(x, g, b):
    mu = jnp.mean(x, axis=-1, keepdims=True)
    xc = x - mu
    var = jnp.mean(xc * xc, axis=-1, keepdims=True)
    return xc * lax.rsqrt(var + LN_EPS) * g + b


def _kv_kernel(mem_ref, wk_ref, wv_ref, k_ref, v_ref):
    m = mem_ref[0].astype(BF16)
    k_ref[0] = _dot(m, wk_ref[...]).astype(BF16)
    v_ref[0] = _dot(m, wv_ref[...]).astype(BF16)


def _kv_proj(mem, w_xk, w_xv):
    B = mem.shape[0]
    const = lambda b: (0, 0)
    return pl.pallas_call(
        _kv_kernel,
        grid=(B,),
        in_specs=[
            pl.BlockSpec((1, N_MEM, D_MODEL), lambda b: (b, 0, 0)),
            pl.BlockSpec((D_MODEL, D_MODEL), const),
            pl.BlockSpec((D_MODEL, D_MODEL), const),
        ],
        out_specs=[
            pl.BlockSpec((1, N_MEM, D_MODEL), lambda b: (b, 0, 0)),
            pl.BlockSpec((1, N_MEM, D_MODEL), lambda b: (b, 0, 0)),
        ],
        out_shape=[jax.ShapeDtypeStruct((B, N_MEM, D_MODEL), BF16)] * 2,
        compiler_params=pltpu.CompilerParams(dimension_semantics=("arbitrary",),
                                             vmem_limit_bytes=VMEM_LIMIT_BYTES),
        name="kv_proj",
    )(mem, w_xk, w_xv)


def _mixer_step(h_next_ref, h_ref, x_keep_ref, x_ref, seq_tile, x_next_ref, w_in_ref, conv_w_ref, conv_b_ref,
                cln_g_ref, cln_b_ref, gn_g_ref, gn_b_ref, w_out_ref, ln1_g_ref, ln1_b_ref, cos_ref, sin_ref,
                decay_ref, qdec_ref, kdec_ref, cdec_ref, o_ref, xb_ref, ubuf_ref, cbuf_ref, state_ref):
    T = MIXER_TILE
    x_next = x_next_ref[0]
    x_keep_ref[...] = x_next
    xb_ref[...] = x_next.astype(BF16)

    def project(p):
        h_next_ref[p] = _dot(xb_ref[...], w_in_ref[p])

    u = h_ref[P_A] * _sigmoid(h_ref[P_B])
    def strip_rows(slab_parity, t_start, n):
        slab, parity = slab_parity
        return slab, pl.ds(2 * t_start + parity, n, stride=2), slice(None)

    for c in range(N_STRIPS):
        ubuf_ref[strip_rows(divmod(c, 2), CONV_HALO, T)] = u[:, c * LANES:(c + 1) * LANES]

    base = CONV_HALO - (CONV_KERNEL - 1)

    def conv_strip(c):
        where = divmod(c, 2)
        n_acc = CONV_ROWS // SUBLANES
        for t0 in range(0, T, CONV_ROWS):
            acc = [jnp.broadcast_to(conv_b_ref[c], (SUBLANES, LANES))] * n_acc
            for r in range(SUBLANES):
                taps = range(r, CONV_KERNEL, SUBLANES)
                wts = [jnp.broadcast_to(conv_w_ref[c, j:j + 1, :], (SUBLANES, LANES)) for j in taps]
                for m in range(n_acc + len(taps) - 1):
                    win = ubuf_ref[strip_rows(where, t0 + base + r + SUBLANES * m, SUBLANES)]
                    for a, wt in enumerate(wts):
                        if 0 <= m - a < n_acc:
                            acc[m - a] = acc[m - a] + win * wt
            cbuf_ref[c, t0:t0 + CONV_ROWS, :] = jnp.concatenate(acc, axis=0)
        project(c)

    for c in range(N_STRIPS):
        pl.when(pl.program_id(0) >= 0)(functools.partial(conv_strip, c))
    ubuf_ref[:, 0:2 * CONV_HALO, :] = ubuf_ref[:, 2 * T:2 * (T + CONV_HALO), :]
    conv = jnp.concatenate([cbuf_ref[c] for c in range(N_STRIPS)], axis=1)
    cn = _layer_norm(conv, cln_g_ref[...], cln_b_ref[...])
    conv_out = (cn * _sigmoid(cn)).astype(BF16)
    for p in range(N_STRIPS, N_PIECES):
        project(p)

    pos_rows = pl.ds(pl.multiple_of(seq_tile * T, T), T)
    cos, sin = cos_ref[pos_rows, :], sin_ref[pos_rows, :]
    half = RET_HEAD_DIM // 2
    C = RET_CHUNK
    n_chunks = T // C
    head_cols = [slice(h * RET_HEAD_DIM, (h + 1) * RET_HEAD_DIM) for h in range(RET_HEADS)]
    q, v, scores, kv = [], [], [], []
    for h, hs in enumerate(head_cols):
        qh = h_ref[P_Q, :, hs]
        kh = h_ref[P_K, :, hs]
        qh = qh * cos + pltpu.roll(qh, half, 1) * sin
        kh = kh * cos + pltpu.roll(kh, half, 1) * sin
        vh = h_ref[P_V, :, hs].astype(BF16)
        q.append(qh)
        v.append(vh)
        decay, kdec = decay_ref[h], kdec_ref[h]
        scores_h, kv_h = [], []
        for c in range(n_chunks):
            rows = slice(c * C, (c + 1) * C)
            scores_h.append((_dot_nt(qh[rows].astype(BF16), kh[rows].astype(BF16)) * decay).astype(BF16))
            kv_h.append(_dot((kh[rows] * kdec).T.astype(BF16), vh[rows]))
        scores.append(scores_h)
        kv.append(kv_h)
    ret_heads = []
    for h, hs in enumerate(head_cols):
        qdec, cdec = qdec_ref[h], cdec_ref[h]
        state = state_ref[h]
        ys = []
        for c in range(n_chunks):
            rows = slice(c * C, (c + 1) * C)
            lhs = jnp.concatenate([scores[h][c], (q[h][rows] * qdec).astype(BF16)], axis=1)
            rhs = jnp.concatenate([v[h][rows], state.astype(BF16)], axis=0)
            ys.append(_dot(lhs, rhs))
            state = state * cdec + kv[h][c]
        state_ref[h] = state
        y = jnp.concatenate(ys, axis=0)
        mu = jnp.mean(y, axis=-1, keepdims=True)
        yc = y - mu
        var = jnp.mean(yc * yc, axis=-1, keepdims=True)
        yn = yc * lax.rsqrt(var + LN_EPS)
        yn = yn * gn_g_ref[:, hs] + gn_b_ref[:, hs]
        gh = h_ref[P_G, :, hs]
        ret_heads.append((gh * _sigmoid(gh) * yn).astype(BF16))

    merged = jnp.concatenate([conv_out] + ret_heads, axis=1)
    mix = _dot(merged, w_out_ref[...])
    o_ref[0] = _layer_norm(DN_ALPHA * x_ref[...] + mix, ln1_g_ref[...], ln1_b_ref[...])


def _mixer_kernel(tiles_per_seq, *refs):
    *io_refs, h_even_ref, h_odd_ref, x_even_ref, x_odd_ref, xb_ref, ubuf_ref, cbuf_ref, state_ref = refs
    n = pl.program_id(0)
    seq_tile = lax.rem(jnp.maximum(n - 1, 0), tiles_per_seq)

    @pl.when(n == 0)
    def _():
        h_odd_ref[...] = jnp.zeros(h_odd_ref.shape, F32)
        x_odd_ref[...] = jnp.zeros(x_odd_ref.shape, F32)

    @pl.when(seq_tile == 0)
    def _():
        ubuf_ref[:, 0:2 * CONV_HALO, :] = jnp.zeros((ubuf_ref.shape[0], 2 * CONV_HALO, LANES), F32)
        state_ref[...] = jnp.zeros(state_ref.shape, F32)

    def step(h_next_ref, h_ref, x_keep_ref, x_ref):
        _mixer_step(h_next_ref, h_ref, x_keep_ref, x_ref, seq_tile, *io_refs, xb_ref, ubuf_ref, cbuf_ref,
                    state_ref)

    @pl.when(lax.rem(n, 2) == 0)
    def _():
        step(h_even_ref, h_odd_ref, x_even_ref, x_odd_ref)

    @pl.when(lax.rem(n, 2) == 1)
    def _():
        step(h_odd_ref, h_even_ref, x_odd_ref, x_even_ref)


def _mixer(x, w_in, conv_w, conv_b, cln_g, cln_b, gn_g, gn_b, w_out, ln1_g, ln1_b, tabs):
    B, S, D = x.shape
    T = MIXER_TILE
    tiles_per_seq = S // T
    n_tiles = B * tiles_per_seq

    def tile_index(tile):
        return lax.div(tile, tiles_per_seq), lax.rem(tile, tiles_per_seq)

    def next_tile(n):
        b, s = tile_index(jnp.minimum(n, n_tiles - 1))
        return b, s, 0

    def this_tile(n):
        b, s = tile_index(jnp.maximum(n - 1, 0))
        return b, s, 0

    const2 = lambda n: (0, 0)
    const3 = lambda n: (0, 0, 0)
    row = lambda width: pl.BlockSpec((1, width), const2)
    pos = lambda: pl.BlockSpec((S, RET_HEAD_DIM), const2, pipeline_mode=pl.Buffered(1))
    tab = lambda: pl.BlockSpec((RET_HEADS, RET_CHUNK, RET_HEAD_DIM), const3)
    w_in_p = w_in.reshape(D, N_PIECES, PROJ_COLS).transpose(1, 0, 2)
    conv_w_s = conv_w.reshape(CONV_KERNEL, N_STRIPS, LANES).transpose(1, 0, 2)
    conv_b_s = conv_b.reshape(N_STRIPS, 1, LANES)
    return pl.pallas_call(
        functools.partial(_mixer_kernel, tiles_per_seq),
        grid=(n_tiles + 1,),
        in_specs=[
            pl.BlockSpec((1, T, D), next_tile),
            pl.BlockSpec((N_PIECES, D, PROJ_COLS), const3, pipeline_mode=pl.Buffered(1)),
            pl.BlockSpec((N_STRIPS, CONV_KERNEL, LANES), const3),
            pl.BlockSpec((N_STRIPS, 1, LANES), const3),
            row(CONV_WIDTH), row(CONV_WIDTH), row(RET_WIDTH), row(RET_WIDTH),
            pl.BlockSpec((D, D), const2, pipeline_mode=pl.Buffered(1)),
            row(D), row(D),
            pos(), pos(),
            tab(), tab(), tab(), tab(),
        ],
        out_specs=pl.BlockSpec((1, T, D), this_tile),
        out_shape=jax.ShapeDtypeStruct((B, S, D), F32),
        scratch_shapes=[
            pltpu.VMEM((N_PIECES, T, PROJ_COLS), F32),
            pltpu.VMEM((N_PIECES, T, PROJ_COLS), F32),
            pltpu.VMEM((T, D), F32),
            pltpu.VMEM((T, D), F32),
            pltpu.VMEM((T, D), BF16),
            pltpu.VMEM((N_STRIPS // 2, 2 * (CONV_HALO + T), LANES), F32),
            pltpu.VMEM((N_STRIPS, T, LANES), F32),
            pltpu.VMEM((RET_HEADS, RET_HEAD_DIM, RET_HEAD_DIM), F32),
        ],
        compiler_params=pltpu.CompilerParams(dimension_semantics=("arbitrary",),
                                             vmem_limit_bytes=VMEM_LIMIT_BYTES),
        name="mixer",
    )(x, w_in_p, conv_w_s, conv_b_s, cln_g, cln_b, gn_g, gn_b, w_out, ln1_g, ln1_b, *tabs)


def _attn_mlp_kernel(x_ref, k_ref, v_ref, wq_ref, wo_ref, ln2_g_ref, ln2_b_ref, wup_ref, wdn_ref,
                     ln3_g_ref, ln3_b_ref, o_ref):
    x = x_ref[0]
    q = (_dot(x.astype(BF16), wq_ref[...]) * (XATTN_HEAD_DIM ** -0.5)).astype(BF16)
    heads = []
    for h in range(XATTN_HEADS):
        lo = h * XATTN_HEAD_DIM
        s = _dot_nt(q[:, lo:lo + XATTN_HEAD_DIM], k_ref[0, :, lo:lo + XATTN_HEAD_DIM])
        p = jnp.exp(s - jnp.max(s, axis=-1, keepdims=True))
        l = jnp.sum(p, axis=-1, keepdims=True)
        o = _dot(p.astype(BF16), v_ref[0, :, lo:lo + XATTN_HEAD_DIM])
        heads.append((o * (1.0 / l)).astype(BF16))
    xa = _dot(jnp.concatenate(heads, axis=1), wo_ref[...])
    x2 = _layer_norm(DN_ALPHA * x + xa, ln2_g_ref[...], ln2_b_ref[...])
    x2b = x2.astype(BF16)
    ff = None
    for j in range(D_FF // FF_CHUNK):
        lo = j * FF_CHUNK
        hmid = jnp.maximum(_dot(x2b, wup_ref[:, lo:lo + FF_CHUNK]), 0.0)
        part = _dot((hmid * hmid).astype(BF16), wdn_ref[lo:lo + FF_CHUNK, :])
        ff = part if ff is None else ff + part
    o_ref[0] = _layer_norm(DN_ALPHA * x2 + ff, ln3_g_ref[...], ln3_b_ref[...])


def _attn_mlp(x1, k, v, w_xq, w_xo, ln2_g, ln2_b, w_up, w_down, ln3_g, ln3_b):
    B, S, D = x1.shape
    T = ATTN_TILE
    const2 = lambda b, s: (0, 0)
    row = lambda n: pl.BlockSpec((1, n), const2)
    weight = lambda shape: pl.BlockSpec(shape, const2, pipeline_mode=pl.Buffered(1))
    return pl.pallas_call(
        _attn_mlp_kernel,
        grid=(B, S // T),
        in_specs=[
            pl.BlockSpec((1, T, D), lambda b, s: (b, s, 0)),
            pl.BlockSpec((1, N_MEM, D), lambda b, s: (b, 0, 0)),
            pl.BlockSpec((1, N_MEM, D), lambda b, s: (b, 0, 0)),
            weight((D, D)), weight((D, D)),
            row(D), row(D),
            weight((D, D_FF)), weight((D_FF, D)),
            row(D), row(D),
        ],
        out_specs=pl.BlockSpec((1, T, D), lambda b, s: (b, s, 0)),
        out_shape=jax.ShapeDtypeStruct((B, S, D), F32),
        compiler_params=pltpu.CompilerParams(dimension_semantics=("arbitrary", "arbitrary"),
                                             vmem_limit_bytes=VMEM_LIMIT_BYTES),
        name="attn_mlp",
    )(x1, k, v, w_xq, w_xo, ln2_g, ln2_b, w_up, w_down, ln3_g, ln3_b)


def _position_tables(seq):
    half = RET_HEAD_DIM // 2
    pos = jnp.arange(seq, dtype=F32)
    inv = ROPE_BASE ** (-jnp.linspace(0.0, 1.0, half, dtype=F32))
    ang = pos[:, None] * inv[None, :]
    cos, sin = jnp.cos(ang), jnp.sin(ang)
    cos2 = jnp.concatenate([cos, cos], axis=-1)
    sin2 = jnp.concatenate([-sin, sin], axis=-1)
    qs = RET_HEAD_DIM ** -0.5
    C = RET_CHUNK
    log_g = jnp.log(1.0 - 2.0 ** (-5.0 - jnp.arange(RET_HEADS, dtype=F32)))
    idx = jnp.arange(C, dtype=F32)
    rel = idx[:, None] - idx[None, :]
    decay = jnp.where(rel >= 0, jnp.exp(log_g[:, None, None] * jnp.maximum(rel, 0.0)), 0.0)
    q_dec = jnp.exp(log_g[:, None] * (idx + 1.0))
    k_dec = jnp.exp(log_g[:, None] * (C - 1.0 - idx))
    chunk_dec = jnp.exp(log_g * C)
    bcast = lambda t: jnp.broadcast_to(t, (RET_HEADS, C, RET_HEAD_DIM))
    return (cos2, sin2, decay * qs,
            bcast(q_dec[:, :, None]) * qs, bcast(k_dec[:, :, None]), bcast(chunk_dec[:, None, None]))


def kernel(x, mem, w_in, conv_w, conv_b, conv_ln_g, conv_ln_b, ret_gn_g, ret_gn_b, w_out,
           ln1_g, ln1_b, w_xq, w_xk, w_xv, w_xo, ln2_g, ln2_b, w_up, w_down, ln3_g, ln3_b):
    assert w_in.shape[0] == DEPTH == 1
    assert RET_CHUNK == RET_HEAD_DIM
    tabs = _position_tables(x.shape[1])
    bf = lambda w: w[0].astype(BF16)
    k, v = _kv_proj(mem, bf(w_xk), bf(w_xv))
    x1 = _mixer(x, bf(w_in), conv_w[0], conv_b, conv_ln_g, conv_ln_b, ret_gn_g, ret_gn_b, bf(w_out),
                ln1_g, ln1_b, tabs)
    return _attn_mlp(x1, k, v, bf(w_xq), bf(w_xo), ln2_g, ln2_b, bf(w_up), bf(w_down), ln3_g, ln3_b)
```

```python
import functools

import jax
import jax.numpy as jnp
from jax import lax
from jax.experimental import pallas as pl
from jax.experimental.pallas import tpu as pltpu

D_MODEL = 1024
N_MEM = 256
CONV_WIDTH = 512
CONV_KERNEL = 31
RET_WIDTH = 512
RET_HEADS = 4
RET_HEAD_DIM = 128
RET_CHUNK = 128
ROPE_BASE = 10000.0
IN_COLS = 2 * CONV_WIDTH + 4 * RET_WIDTH
XATTN_HEADS = 4
XATTN_HEAD_DIM = 256
D_FF = 4 * D_MODEL
LN_EPS = 1e-5
DEPTH = 1
DN_ALPHA = (2.0 * DEPTH) ** 0.25

CONV_HALO = 32
SUBLANES, LANES = 8, 128
CONV_ROWS = 128
N_STRIPS = CONV_WIDTH // LANES
PROJ_COLS = 512
N_PIECES = IN_COLS // PROJ_COLS
P_A, P_B, P_Q, P_K, P_V, P_G = range(N_PIECES)
MIXER_TILE = 256
ATTN_TILE = 512
FF_CHUNK = 1024
VMEM_LIMIT_BYTES = 56 * 1024 * 1024

BF16 = jnp.bfloat16
F32 = jnp.float32


def _dot(a, b):
    return jnp.dot(a, b, preferred_element_type=F32)


def _dot_nt(a, b):
    return lax.dot_general(a, b, (((1,), (1,)), ((), ())), preferred_element_type=F32)


def _sigmoid(x):
    return 1.0 / (1.0 + jnp.exp(-x))


def _layer_norm(x, g, b):
    mu = jnp.mean(x, axis=-1, keepdims=True)
    xc = x - mu
    var = jnp.mean(xc * xc, axis=-1, keepdims=True)
    return xc * lax.rsqrt(var + LN_EPS) * g + b


def _kv_kernel(mem_ref, wk_ref, wv_ref, k_ref, v_ref):
    m = mem_ref[0].astype(BF16)
    k_ref[0] = _dot(m, wk_ref[...]).astype(BF16)
    v_ref[0] = _dot(m, wv_ref[...]).astype(BF16)


def _kv_proj(mem, w_xk, w_xv):
    B = mem.shape[0]
    const = lambda b: (0, 0)
    return pl.pallas_call(
        _kv_kernel,
        grid=(B,),
        in_specs=[
            pl.BlockSpec((1, N_MEM, D_MODEL), lambda b: (b, 0, 0)),
            pl.BlockSpec((D_MODEL, D_MODEL), const),
            pl.BlockSpec((D_MODEL, D_MODEL), const),
        ],
        out_specs=[
            pl.BlockSpec((1, N_MEM, D_MODEL), lambda b: (b, 0, 0)),
            pl.BlockSpec((1, N_MEM, D_MODEL), lambda b: (b, 0, 0)),
        ],
        out_shape=[jax.ShapeDtypeStruct((B, N_MEM, D_MODEL), BF16)] * 2,
        compiler_params=pltpu.CompilerParams(dimension_semantics=("arbitrary",),
                                             vmem_limit_bytes=VMEM_LIMIT_BYTES),
        name="kv_proj",
    )(mem, w_xk, w_xv)


def _mixer_step(h_next_ref, h_ref, x_keep_ref, x_ref, seq_tile, x_next_ref, w_in_ref, conv_w_ref, conv_b_ref,
                cln_g_ref, cln_b_ref, gn_g_ref, gn_b_ref, w_out_ref, ln1_g_ref, ln1_b_ref, cos_ref, sin_ref,
                decay_ref, qdec_ref, kdec_ref, cdec_ref, o_ref, xb_ref, ubuf_ref, cbuf_ref, state_ref):
    T = MIXER_TILE
    x_next = x_next_ref[0]
    x_keep_ref[...] = x_next
    xb_ref[...] = x_next.astype(BF16)

    def project(p):
        h_next_ref[p] = _dot(xb_ref[...], w_in_ref[p])

    u = h_ref[P_A] * _sigmoid(h_ref[P_B])
    def strip_rows(slab_parity, t_start, n):
        slab, parity = slab_parity
        return slab, pl.ds(2 * t_start + parity, n, stride=2), slice(None)

    for c in range(N_STRIPS):
        ubuf_ref[strip_rows(divmod(c, 2), CONV_HALO, T)] = u[:, c * LANES:(c + 1) * LANES]

    base = CONV_HALO - (CONV_KERNEL - 1)

    def conv_strip(c):
        where = divmod(c, 2)
        n_acc = CONV_ROWS // SUBLANES
        for t0 in range(0, T, CONV_ROWS):
            acc = [jnp.broadcast_to(conv_b_ref[c], (SUBLANES, LANES))] * n_acc
            for r in range(SUBLANES):
                taps = range(r, CONV_KERNEL, SUBLANES)
                wts = [jnp.broadcast_to(conv_w_ref[c, j:j + 1, :], (SUBLANES, LANES)) for j in taps]
                for m in range(n_acc + len(taps) - 1):
                    win = ubuf_ref[strip_rows(where, t0 + base + r + SUBLANES * m, SUBLANES)]
                    for a, wt in enumerate(wts):
                        if 0 <= m - a < n_acc:
                            acc[m - a] = acc[m - a] + win * wt
            cbuf_ref[c, t0:t0 + CONV_ROWS, :] = jnp.concatenate(acc, axis=0)
        project(c)

    for c in range(N_STRIPS):
        pl.when(pl.program_id(0) >= 0)(functools.partial(conv_strip, c))
    ubuf_ref[:, 0:2 * CONV_HALO, :] = ubuf_ref[:, 2 * T:2 * (T + CONV_HALO), :]
    conv = jnp.concatenate([cbuf_ref[c] for c in range(N_STRIPS)], axis=1)
    cn = _layer_norm(conv, cln_g_ref[...], cln_b_ref[...])
    conv_out = (cn * _sigmoid(cn)).astype(BF16)
    for p in range(N_STRIPS, N_PIECES):
        project(p)

    pos_rows = pl.ds(pl.multiple_of(seq_tile * T, T), T)
    cos, sin = cos_ref[pos_rows, :], sin_ref[pos_rows, :]
    half = RET_HEAD_DIM // 2
    C = RET_CHUNK
    n_chunks = T // C
    head_cols = [slice(h * RET_HEAD_DIM, (h + 1) * RET_HEAD_DIM) for h in range(RET_HEADS)]
    q, v, scores, kv = [], [], [], []
    for h, hs in enumerate(head_cols):
        qh = h_ref[P_Q, :, hs]
        kh = h_ref[P_K, :, hs]
        qh = qh * cos + pltpu.roll(qh, half, 1) * sin
        kh = kh * cos + pltpu.roll(kh, half, 1) * sin
        vh = h_ref[P_V, :, hs].astype(BF16)
        q.append(qh)
        v.append(vh)
        decay, kdec = decay_ref[h], kdec_ref[h]
        scores_h, kv_h = [], []
        for c in range(n_chunks):
            rows = slice(c * C, (c + 1) * C)
            scores_h.append((_dot_nt(qh[rows].astype(BF16), kh[rows].astype(BF16)) * decay).astype(BF16))
            kv_h.append(_dot((kh[rows] * kdec).T.astype(BF16), vh[rows]))
        scores.append(scores_h)
        kv.append(kv_h)
    ret_heads = []
    for h, hs in enumerate(head_cols):
        qdec, cdec = qdec_ref[h], cdec_ref[h]
        state = state_ref[h]
        ys = []
        for c in range(n_chunks):
            rows = slice(c * C, (c + 1) * C)
            lhs = jnp.concatenate([scores[h][c], (q[h][rows] * qdec).astype(BF16)], axis=1)
            rhs = jnp.concatenate([v[h][rows], state.astype(BF16)], axis=0)
            ys.append(_dot(lhs, rhs))
            state = state * cdec + kv[h][c]
        state_ref[h] = state
        y = jnp.concatenate(ys, axis=0)
        mu = jnp.mean(y, axis=-1, keepdims=True)
        yc = y - mu
        var = jnp.mean(yc * yc, axis=-1, keepdims=True)
        yn = yc * lax.rsqrt(var + LN_EPS)
        yn = yn * gn_g_ref[:, hs] + gn_b_ref[:, hs]
        gh = h_ref[P_G, :, hs]
        ret_heads.append((gh * _sigmoid(gh) * yn).astype(BF16))

    merged = jnp.concatenate([conv_out] + ret_heads, axis=1)
    mix = _dot(merged, w_out_ref[...])
    o_ref[0] = _layer_norm(DN_ALPHA * x_ref[...] + mix, ln1_g_ref[...], ln1_b_ref[...])


def _mixer_kernel(tiles_per_seq, *refs):
    *io_refs, h_even_ref, h_odd_ref, x_even_ref, x_odd_ref, xb_ref, ubuf_ref, cbuf_ref, state_ref = refs
    n = pl.program_id(0)
    seq_tile = lax.rem(jnp.maximum(n - 1, 0), tiles_per_seq)

    @pl.when(n == 0)
    def _():
        h_odd_ref[...] = jnp.zeros(h_odd_ref.shape, F32)
        x_odd_ref[...] = jnp.zeros(x_odd_ref.shape, F32)

    @pl.when(seq_tile == 0)
    def _():
        ubuf_ref[:, 0:2 * CONV_HALO, :] = jnp.zeros((ubuf_ref.shape[0], 2 * CONV_HALO, LANES), F32)
        state_ref[...] = jnp.zeros(state_ref.shape, F32)

    def step(h_next_ref, h_ref, x_keep_ref, x_ref):
        _mixer_step(h_next_ref, h_ref, x_keep_ref, x_ref, seq_tile, *io_refs, xb_ref, ubuf_ref, cbuf_ref,
                    state_ref)

    @pl.when(lax.rem(n, 2) == 0)
    def _():
        step(h_even_ref, h_odd_ref, x_even_ref, x_odd_ref)

    @pl.when(lax.rem(n, 2) == 1)
    def _():
        step(h_odd_ref, h_even_ref, x_odd_ref, x_even_ref)


def _mixer(x, w_in, conv_w, conv_b, cln_g, cln_b, gn_g, gn_b, w_out, ln1_g, ln1_b, tabs):
    B, S, D = x.shape
    T = MIXER_TILE
    tiles_per_seq = S // T
    n_tiles = B * tiles_per_seq

    def tile_index(tile):
        return lax.div(tile, tiles_per_seq), lax.rem(tile, tiles_per_seq)

    def next_tile(n):
        b, s = tile_index(jnp.minimum(n, n_tiles - 1))
        return b, s, 0

    def this_tile(n):
        b, s = tile_index(jnp.maximum(n - 1, 0))
        return b, s, 0

    const2 = lambda n: (0, 0)
    const3 = lambda n: (0, 0, 0)
    row = lambda width: pl.BlockSpec((1, width), const2)
    pos = lambda: pl.BlockSpec((S, RET_HEAD_DIM), const2, pipeline_mode=pl.Buffered(1))
    tab = lambda: pl.BlockSpec((RET_HEADS, RET_CHUNK, RET_HEAD_DIM), const3)
    w_in_p = w_in.reshape(D, N_PIECES, PROJ_COLS).transpose(1, 0, 2)
    conv_w_s = conv_w.reshape(CONV_KERNEL, N_STRIPS, LANES).transpose(1, 0, 2)
    conv_b_s = conv_b.reshape(N_STRIPS, 1, LANES)
    return pl.pallas_call(
        functools.partial(_mixer_kernel, tiles_per_seq),
        grid=(n_tiles + 1,),
        in_specs=[
            pl.BlockSpec((1, T, D), next_tile),
            pl.BlockSpec((N_PIECES, D, PROJ_COLS), const3, pipeline_mode=pl.Buffered(1)),
            pl.BlockSpec((N_STRIPS, CONV_KERNEL, LANES), const3),
            pl.BlockSpec((N_STRIPS, 1, LANES), const3),
            row(CONV_WIDTH), row(CONV_WIDTH), row(RET_WIDTH), row(RET_WIDTH),
            pl.BlockSpec((D, D), const2, pipeline_mode=pl.Buffered(1)),
            row(D), row(D),
            pos(), pos(),
            tab(), tab(), tab(), tab(),
        ],
        out_specs=pl.BlockSpec((1, T, D), this_tile),
        out_shape=jax.ShapeDtypeStruct((B, S, D), F32),
        scratch_shapes=[
            pltpu.VMEM((N_PIECES, T, PROJ_COLS), F32),
            pltpu.VMEM((N_PIECES, T, PROJ_COLS), F32),
            pltpu.VMEM((T, D), F32),
            pltpu.VMEM((T, D), F32),
            pltpu.VMEM((T, D), BF16),
            pltpu.VMEM((N_STRIPS // 2, 2 * (CONV_HALO + T), LANES), F32),
            pltpu.VMEM((N_STRIPS, T, LANES), F32),
            pltpu.VMEM((RET_HEADS, RET_HEAD_DIM, RET_HEAD_DIM), F32),
        ],
        compiler_params=pltpu.CompilerParams(dimension_semantics=("arbitrary",),
                                             vmem_limit_bytes=VMEM_LIMIT_BYTES),
        name="mixer",
    )(x, w_in_p, conv_w_s, conv_b_s, cln_g, cln_b, gn_g, gn_b, w_out, ln1_g, ln1_b, *tabs)


def _attn_mlp_kernel(x_ref, k_ref, v_ref, wq_ref, wo_ref, ln2_g_ref, ln2_b_ref, wup_ref, wdn_ref,
                     ln3_g_ref, ln3_b_ref, o_ref):
    x = x_ref[0]
    q = (_dot(x.astype(BF16), wq_ref[...]) * (XATTN_HEAD_DIM ** -0.5)).astype(BF16)
    heads = []
    for h in range(XATTN_HEADS):
        lo = h * XATTN_HEAD_DIM
        s = _dot_nt(q[:, lo:lo + XATTN_HEAD_DIM], k_ref[0, :, lo:lo + XATTN_HEAD_DIM])
        p = jnp.exp(s - jnp.max(s, axis=-1, keepdims=True))
        l = jnp.sum(p, axis=-1, keepdims=True)
        o = _dot(p.astype(BF16), v_ref[0, :, lo:lo + XATTN_HEAD_DIM])
        heads.append((o * (1.0 / l)).astype(BF16))
    xa = _dot(jnp.concatenate(heads, axis=1), wo_ref[...])
    x2 = _layer_norm(DN_ALPHA * x + xa, ln2_g_ref[...], ln2_b_ref[...])
    x2b = x2.astype(BF16)
    ff = None
    for j in range(D_FF // FF_CHUNK):
        lo = j * FF_CHUNK
        hmid = jnp.maximum(_dot(x2b, wup_ref[:, lo:lo + FF_CHUNK]), 0.0)
        part = _dot((hmid * hmid).astype(BF16), wdn_ref[lo:lo + FF_CHUNK, :])
        ff = part if ff is None else ff + part
    o_ref[0] = _layer_norm(DN_ALPHA * x2 + ff, ln3_g_ref[...], ln3_b_ref[...])


def _attn_mlp(x1, k, v, w_xq, w_xo, ln2_g, ln2_b, w_up, w_down, ln3_g, ln3_b):
    B, S, D = x1.shape
    T = ATTN_TILE
    const2 = lambda b, s: (0, 0)
    row = lambda n: pl.BlockSpec((1, n), const2)
    weight = lambda shape: pl.BlockSpec(shape, const2, pipeline_mode=pl.Buffered(1))
    return pl.pallas_call(
        _attn_mlp_kernel,
        grid=(B, S // T),
        in_specs=[
            pl.BlockSpec((1, T, D), lambda b, s: (b, s, 0)),
            pl.BlockSpec((1, N_MEM, D), lambda b, s: (b, 0, 0)),
            pl.BlockSpec((1, N_MEM, D), lambda b, s: (b, 0, 0)),
            weight((D, D)), weight((D, D)),
            row(D), row(D),
            weight((D, D_FF)), weight((D_FF, D)),
            row(D), row(D),
        ],
        out_specs=pl.BlockSpec((1, T, D), lambda b, s: (b, s, 0)),
        out_shape=jax.ShapeDtypeStruct((B, S, D), F32),
        compiler_params=pltpu.CompilerParams(dimension_semantics=("arbitrary", "arbitrary"),
                                             vmem_limit_bytes=VMEM_LIMIT_BYTES),
        name="attn_mlp",
    )(x1, k, v, w_xq, w_xo, ln2_g, ln2_b, w_up, w_down, ln3_g, ln3_b)


def _position_tables(seq):
    half = RET_HEAD_DIM // 2
    pos = jnp.arange(seq, dtype=F32)
    inv = ROPE_BASE ** (-jnp.linspace(0.0, 1.0, half, dtype=F32))
    ang = pos[:, None] * inv[None, :]
    cos, sin = jnp.cos(ang), jnp.sin(ang)
    cos2 = jnp.concatenate([cos, cos], axis=-1)
    sin2 = jnp.concatenate([-sin, sin], axis=-1)
    qs = RET_HEAD_DIM ** -0.5
    C = RET_CHUNK
    log_g = jnp.log(1.0 - 2.0 ** (-5.0 - jnp.arange(RET_HEADS, dtype=F32)))
    idx = jnp.arange(C, dtype=F32)
    rel = idx[:, None] - idx[None, :]
    decay = jnp.where(rel >= 0, jnp.exp(log_g[:, None, None] * jnp.maximum(rel, 0.0)), 0.0)
    q_dec = jnp.exp(log_g[:, None] * (idx + 1.0))
    k_dec = jnp.exp(log_g[:, None] * (C - 1.0 - idx))
    chunk_dec = jnp.exp(log_g * C)
    bcast = lambda t: jnp.broadcast_to(t, (RET_HEADS, C, RET_HEAD_DIM))
    return (cos2, sin2, decay * qs,
            bcast(q_dec[:, :, None]) * qs, bcast(k_dec[:, :, None]), bcast(chunk_dec[:, None, None]))


def kernel(x, mem, w_in, conv_w, conv_b, conv_ln_g, conv_ln_b, ret_gn_g, ret_gn_b, w_out,
           ln1_g, ln1_b, w_xq, w_xk, w_xv, w_xo, ln2_g, ln2_b, w_up, w_down, ln3_g, ln3_b):
    assert w_in.shape[0] == DEPTH == 1
    assert RET_CHUNK == RET_HEAD_DIM
    tabs = _position_tables(x.shape[1])
    bf = lambda w: w[0].astype(BF16)
    k, v = _kv_proj(mem, bf(w_xk), bf(w_xv))
    x1 = _mixer(x, bf(w_in), conv_w[0], conv_b, conv_ln_g, conv_ln_b, ret_gn_g, ret_gn_b, bf(w_out),
                ln1_g, ln1_b, tabs)
    return _attn_mlp(x1, k, v, bf(w_xq), bf(w_xo), ln2_g, ln2_b, bf(w_up), bf(w_down), ln3_g, ln3_b)
```

```python
import functools

import jax
import jax.numpy as jnp
from jax import lax
from jax.experimental import pallas as pl
from jax.experimental.pallas import tpu as pltpu

D_MODEL = 1024
N_MEM = 256
CONV_WIDTH = 512
CONV_KERNEL = 31
RET_WIDTH = 512
RET_HEADS = 4
RET_HEAD_DIM = 128
RET_CHUNK = 128
ROPE_BASE = 10000.0
IN_COLS = 2 * CONV_WIDTH + 4 * RET_WIDTH
XATTN_HEADS = 4
XATTN_HEAD_DIM = 256
D_FF = 4 * D_MODEL
LN_EPS = 1e-5
DEPTH = 1
DN_ALPHA = (2.0 * DEPTH) ** 0.25

CONV_HALO = 32
SUBLANES, LANES = 8, 128
CONV_ROWS = 128
N_STRIPS = CONV_WIDTH // LANES
PROJ_COLS = 512
N_PIECES = IN_COLS // PROJ_COLS
P_A, P_B, P_Q, P_K, P_V, P_G = range(N_PIECES)
MIXER_TILE = 512
ATTN_TILE = 512
FF_CHUNK = 1024
VMEM_LIMIT_BYTES = 56 * 1024 * 1024

BF16 = jnp.bfloat16
F32 = jnp.float32


def _dot(a, b):
    return jnp.dot(a, b, preferred_element_type=F32)


def _dot_nt(a, b):
    return lax.dot_general(a, b, (((1,), (1,)), ((), ())), preferred_element_type=F32)


def _sigmoid(x):
    return 1.0 / (1.0 + jnp.exp(-x))


def _layer_norm(x, g, b):
    mu = jnp.mean(x, axis=-1, keepdims=True)
    xc = x - mu
    var = jnp.mean(xc * xc, axis=-1, keepdims=True)
    return xc * lax.rsqrt(var + LN_EPS) * g + b


def _kv_kernel(mem_ref, wk_ref, wv_ref, k_ref, v_ref):
    m = mem_ref[0].astype(BF16)
    k_ref[0] = _dot(m, wk_ref[...]).astype(BF16)
    v_ref[0] = _dot(m, wv_ref[...]).astype(BF16)


def _kv_proj(mem, w_xk, w_xv):
    B = mem.shape[0]
    const = lambda b: (0, 0)
    return pl.pallas_call(
        _kv_kernel,
        grid=(B,),
        in_specs=[
            pl.BlockSpec((1, N_MEM, D_MODEL), lambda b: (b, 0, 0)),
            pl.BlockSpec((D_MODEL, D_MODEL), const),
            pl.BlockSpec((D_MODEL, D_MODEL), const),
        ],
        out_specs=[
            pl.BlockSpec((1, N_MEM, D_MODEL), lambda b: (b, 0, 0)),
            pl.BlockSpec((1, N_MEM, D_MODEL), lambda b: (b, 0, 0)),
        ],
        out_shape=[jax.ShapeDtypeStruct((B, N_MEM, D_MODEL), BF16)] * 2,
        compiler_params=pltpu.CompilerParams(dimension_semantics=("arbitrary",),
                                             vmem_limit_bytes=VMEM_LIMIT_BYTES),
        name="kv_proj",
    )(mem, w_xk, w_xv)


def _mixer_step(h_next_ref, h_ref, x_keep_ref, x_ref, seq_tile, x_next_ref, w_in_ref, conv_w_ref, conv_b_ref,
                cln_g_ref, cln_b_ref, gn_g_ref, gn_b_ref, w_out_ref, ln1_g_ref, ln1_b_ref, cos_ref, sin_ref,
                decay_ref, qdec_ref, kdec_ref, cdec_ref, o_ref, xb_ref, ubuf_ref, cbuf_ref, state_ref):
    T = MIXER_TILE
    x_next = x_next_ref[0]
    x_keep_ref[...] = x_next
    xb_ref[...] = x_next.astype(BF16)

    def project(p):
        h_next_ref[p] = _dot(xb_ref[...], w_in_ref[:, p * PROJ_COLS:(p + 1) * PROJ_COLS])

    u = h_ref[P_A] * _sigmoid(h_ref[P_B])
    def strip_rows(slab_parity, t_start, n):
        slab, parity = slab_parity
        return slab, pl.ds(2 * t_start + parity, n, stride=2), slice(None)

    for c in range(N_STRIPS):
        ubuf_ref[strip_rows(divmod(c, 2), CONV_HALO, T)] = u[:, c * LANES:(c + 1) * LANES]

    base = CONV_HALO - (CONV_KERNEL - 1)

    def conv_strip(c):
        where = divmod(c, 2)
        n_acc = CONV_ROWS // SUBLANES
        for t0 in range(0, T, CONV_ROWS):
            acc = [jnp.broadcast_to(conv_b_ref[c], (SUBLANES, LANES))] * n_acc
            for r in range(SUBLANES):
                taps = range(r, CONV_KERNEL, SUBLANES)
                wts = [jnp.broadcast_to(conv_w_ref[c, j:j + 1, :], (SUBLANES, LANES)) for j in taps]
                for m in range(n_acc + len(taps) - 1):
                    win = ubuf_ref[strip_rows(where, t0 + base + r + SUBLANES * m, SUBLANES)]
                    for a, wt in enumerate(wts):
                        if 0 <= m - a < n_acc:
                            acc[m - a] = acc[m - a] + win * wt
            cbuf_ref[c, t0:t0 + CONV_ROWS, :] = jnp.concatenate(acc, axis=0)

    for c in range(N_STRIPS):
        pl.when(pl.program_id(0) >= 0)(functools.partial(conv_strip, c))
    ubuf_ref[:, 0:2 * CONV_HALO, :] = ubuf_ref[:, 2 * T:2 * (T + CONV_HALO), :]
    project(0)
    conv = jnp.concatenate([cbuf_ref[c] for c in range(N_STRIPS)], axis=1)
    cn = _layer_norm(conv, cln_g_ref[...], cln_b_ref[...])
    conv_out = (cn * _sigmoid(cn)).astype(BF16)
    mix_conv = _dot(conv_out, w_out_ref[0:CONV_WIDTH, :])

    pos_rows = pl.ds(pl.multiple_of(seq_tile * T, T), T)
    cos, sin = cos_ref[pos_rows, :], sin_ref[pos_rows, :]
    half = RET_HEAD_DIM // 2
    C = RET_CHUNK
    n_chunks = T // C
    head_cols = [slice(h * RET_HEAD_DIM, (h + 1) * RET_HEAD_DIM) for h in range(RET_HEADS)]
    q, v, scores, kv = [], [], [], []
    project(1)
    for h, hs in enumerate(head_cols):
        qh = h_ref[P_Q, :, hs]
        kh = h_ref[P_K, :, hs]
        qh = qh * cos + pltpu.roll(qh, half, 1) * sin
        kh = kh * cos + pltpu.roll(kh, half, 1) * sin
        vh = h_ref[P_V, :, hs].astype(BF16)
        q.append(qh)
        v.append(vh)
        decay, kdec = decay_ref[h], kdec_ref[h]
        scores_h, kv_h = [], []
        for c in range(n_chunks):
            rows = slice(c * C, (c + 1) * C)
            scores_h.append((_dot_nt(qh[rows].astype(BF16), kh[rows].astype(BF16)) * decay).astype(BF16))
            kv_h.append(_dot((kh[rows] * kdec).T.astype(BF16), vh[rows]))
        scores.append(scores_h)
        kv.append(kv_h)
    project(2)
    ret_heads = []
    for h, hs in enumerate(head_cols):
        qdec, cdec = qdec_ref[h], cdec_ref[h]
        state = state_ref[h]
        ys = []
        for c in range(n_chunks):
            rows = slice(c * C, (c + 1) * C)
            lhs = jnp.concatenate([scores[h][c], (q[h][rows] * qdec).astype(BF16)], axis=1)
            rhs = jnp.concatenate([v[h][rows], state.astype(BF16)], axis=0)
            ys.append(_dot(lhs, rhs))
            state = state * cdec + kv[h][c]
        state_ref[h] = state
        y = jnp.concatenate(ys, axis=0)
        mu = jnp.mean(y, axis=-1, keepdims=True)
        yc = y - mu
        var = jnp.mean(yc * yc, axis=-1, keepdims=True)
        yn = yc * lax.rsqrt(var + LN_EPS)
        yn = yn * gn_g_ref[:, hs] + gn_b_ref[:, hs]
        gh = h_ref[P_G, :, hs]
        ret_heads.append((gh * _sigmoid(gh) * yn).astype(BF16))

    project(3)
    project(4)
    ret_out = jnp.concatenate(ret_heads, axis=1)
    mix = mix_conv + _dot(ret_out, w_out_ref[CONV_WIDTH:, :])
    project(5)
    o_ref[0] = _layer_norm(DN_ALPHA * x_ref[...] + mix, ln1_g_ref[...], ln1_b_ref[...])


def _mixer_kernel(tiles_per_seq, *refs):
    *io_refs, h_even_ref, h_odd_ref, x_even_ref, x_odd_ref, xb_ref, ubuf_ref, cbuf_ref, state_ref = refs
    n = pl.program_id(0)
    seq_tile = lax.rem(jnp.maximum(n - 1, 0), tiles_per_seq)

    @pl.when(n == 0)
    def _():
        h_odd_ref[...] = jnp.zeros(h_odd_ref.shape, F32)
        x_odd_ref[...] = jnp.zeros(x_odd_ref.shape, F32)

    @pl.when(seq_tile == 0)
    def _():
        ubuf_ref[:, 0:2 * CONV_HALO, :] = jnp.zeros((ubuf_ref.shape[0], 2 * CONV_HALO, LANES), F32)
        state_ref[...] = jnp.zeros(state_ref.shape, F32)

    def step(h_next_ref, h_ref, x_keep_ref, x_ref):
        _mixer_step(h_next_ref, h_ref, x_keep_ref, x_ref, seq_tile, *io_refs, xb_ref, ubuf_ref, cbuf_ref,
                    state_ref)

    @pl.when(lax.rem(n, 2) == 0)
    def _():
        step(h_even_ref, h_odd_ref, x_even_ref, x_odd_ref)

    @pl.when(lax.rem(n, 2) == 1)
    def _():
        step(h_odd_ref, h_even_ref, x_odd_ref, x_even_ref)


def _mixer(x, w_in, conv_w, conv_b, cln_g, cln_b, gn_g, gn_b, w_out, ln1_g, ln1_b, tabs):
    B, S, D = x.shape
    T = MIXER_TILE
    tiles_per_seq = S // T
    n_tiles = B * tiles_per_seq

    def tile_index(tile):
        return lax.div(tile, tiles_per_seq), lax.rem(tile, tiles_per_seq)

    def next_tile(n):
        b, s = tile_index(jnp.minimum(n, n_tiles - 1))
        return b, s, 0

    def this_tile(n):
        b, s = tile_index(jnp.maximum(n - 1, 0))
        return b, s, 0

    const2 = lambda n: (0, 0)
    const3 = lambda n: (0, 0, 0)
    row = lambda width: pl.BlockSpec((1, width), const2)
    pos = lambda: pl.BlockSpec((S, RET_HEAD_DIM), const2, pipeline_mode=pl.Buffered(1))
    tab = lambda: pl.BlockSpec((RET_HEADS, RET_CHUNK, RET_HEAD_DIM), const3)
    conv_w_s = conv_w.reshape(CONV_KERNEL, N_STRIPS, LANES).transpose(1, 0, 2)
    conv_b_s = conv_b.reshape(N_STRIPS, 1, LANES)
    return pl.pallas_call(
        functools.partial(_mixer_kernel, tiles_per_seq),
        grid=(n_tiles + 1,),
        in_specs=[
            pl.BlockSpec((1, T, D), next_tile),
            pl.BlockSpec((D, IN_COLS), const2, pipeline_mode=pl.Buffered(1)),
            pl.BlockSpec((N_STRIPS, CONV_KERNEL, LANES), const3),
            pl.BlockSpec((N_STRIPS, 1, LANES), const3),
            row(CONV_WIDTH), row(CONV_WIDTH), row(RET_WIDTH), row(RET_WIDTH),
            pl.BlockSpec((D, D), const2, pipeline_mode=pl.Buffered(1)),
            row(D), row(D),
            pos(), pos(),
            tab(), tab(), tab(), tab(),
        ],
        out_specs=pl.BlockSpec((1, T, D), this_tile),
        out_shape=jax.ShapeDtypeStruct((B, S, D), F32),
        scratch_shapes=[
            pltpu.VMEM((N_PIECES, T, PROJ_COLS), F32),
            pltpu.VMEM((N_PIECES, T, PROJ_COLS), F32),
            pltpu.VMEM((T, D), F32),
            pltpu.VMEM((T, D), F32),
            pltpu.VMEM((T, D), BF16),
            pltpu.VMEM((N_STRIPS // 2, 2 * (CONV_HALO + T), LANES), F32),
            pltpu.VMEM((N_STRIPS, T, LANES), F32),
            pltpu.VMEM((RET_HEADS, RET_HEAD_DIM, RET_HEAD_DIM), F32),
        ],
        compiler_params=pltpu.CompilerParams(dimension_semantics=("arbitrary",),
                                             vmem_limit_bytes=VMEM_LIMIT_BYTES),
        name="mixer",
    )(x, w_in, conv_w_s, conv_b_s, cln_g, cln_b, gn_g, gn_b, w_out, ln1_g, ln1_b, *tabs)


def _attn_mlp_kernel(x_ref, k_ref, v_ref, wq_ref, wo_ref, ln2_g_ref, ln2_b_ref, wup_ref, wdn_ref,
                     ln3_g_ref, ln3_b_ref, o_ref):
    x = x_ref[0]
    T = x.shape[0]
    halves = (slice(0, T // 2), slice(T // 2, T))
    q = (_dot(x.astype(BF16), wq_ref[...]) * (XATTN_HEAD_DIM ** -0.5)).astype(BF16)
    head_cols = [slice(h * XATTN_HEAD_DIM, (h + 1) * XATTN_HEAD_DIM) for h in range(XATTN_HEADS)]
    scores = [_dot_nt(q[:, hs], k_ref[0, :, hs]) for hs in head_cols]
    heads = []
    for s, hs in zip(scores, head_cols):
        p = jnp.exp(s - jnp.max(s, axis=-1, keepdims=True))
        l = jnp.sum(p, axis=-1, keepdims=True)
        o = _dot(p.astype(BF16), v_ref[0, :, hs])
        heads.append((o * (1.0 / l)).astype(BF16))
    attn = jnp.concatenate(heads, axis=1)
    xa = [_dot(attn[rows], wo_ref[...]) for rows in halves]
    x2 = [_layer_norm(DN_ALPHA * x[rows] + xa_r, ln2_g_ref[...], ln2_b_ref[...]) for rows, xa_r in zip(halves, xa)]
    x2b_halves = [x2_r.astype(BF16) for x2_r in x2]
    n_chunks = D_FF // FF_CHUNK
    up_cols = [slice(j * FF_CHUNK, (j + 1) * FF_CHUNK) for j in range(n_chunks)]

    def act(up):
        r = jnp.maximum(up, 0.0)
        return (r * r).astype(BF16)

    up = jnp.concatenate([_dot(x2b_r, wup_ref[:, up_cols[0]]) for x2b_r in x2b_halves], axis=0)
    x2b = jnp.concatenate(x2b_halves, axis=0)
    ff = None
    for j in range(1, n_chunks):
        up_next = _dot(x2b, wup_ref[:, up_cols[j]])
        part = _dot(act(up), wdn_ref[up_cols[j - 1], :])
        ff = part if ff is None else ff + part
        up = up_next
    hid = act(up)
    for rows, x2_r in zip(halves, x2):
        ff_r = ff[rows] + _dot(hid[rows], wdn_ref[up_cols[-1], :])
        o_ref[0, rows, :] = _layer_norm(DN_ALPHA * x2_r + ff_r, ln3_g_ref[...], ln3_b_ref[...])


def _attn_mlp(x1, k, v, w_xq, w_xo, ln2_g, ln2_b, w_up, w_down, ln3_g, ln3_b):
    B, S, D = x1.shape
    T = ATTN_TILE
    const2 = lambda b, s: (0, 0)
    row = lambda n: pl.BlockSpec((1, n), const2)
    weight = lambda shape: pl.BlockSpec(shape, const2, pipeline_mode=pl.Buffered(1))
    return pl.pallas_call(
        _attn_mlp_kernel,
        grid=(B, S // T),
        in_specs=[
            pl.BlockSpec((1, T, D), lambda b, s: (b, s, 0)),
            pl.BlockSpec((1, N_MEM, D), lambda b, s: (b, 0, 0)),
            pl.BlockSpec((1, N_MEM, D), lambda b, s: (b, 0, 0)),
            weight((D, D)), weight((D, D)),
            row(D), row(D),
            weight((D, D_FF)), weight((D_FF, D)),
            row(D), row(D),
        ],
        out_specs=pl.BlockSpec((1, T, D), lambda b, s: (b, s, 0)),
        out_shape=jax.ShapeDtypeStruct((B, S, D), F32),
        compiler_params=pltpu.CompilerParams(dimension_semantics=("arbitrary", "arbitrary"),
                                             vmem_limit_bytes=VMEM_LIMIT_BYTES),
        name="attn_mlp",
    )(x1, k, v, w_xq, w_xo, ln2_g, ln2_b, w_up, w_down, ln3_g, ln3_b)


def _position_tables(seq):
    half = RET_HEAD_DIM // 2
    pos = jnp.arange(seq, dtype=F32)
    inv = ROPE_BASE ** (-jnp.linspace(0.0, 1.0, half, dtype=F32))
    ang = pos[:, None] * inv[None, :]
    cos, sin = jnp.cos(ang), jnp.sin(ang)
    cos2 = jnp.concatenate([cos, cos], axis=-1)
    sin2 = jnp.concatenate([-sin, sin], axis=-1)
    qs = RET_HEAD_DIM ** -0.5
    C = RET_CHUNK
    log_g = jnp.log(1.0 - 2.0 ** (-5.0 - jnp.arange(RET_HEADS, dtype=F32)))
    idx = jnp.arange(C, dtype=F32)
    rel = idx[:, None] - idx[None, :]
    decay = jnp.where(rel >= 0, jnp.exp(log_g[:, None, None] * jnp.maximum(rel, 0.0)), 0.0)
    q_dec = jnp.exp(log_g[:, None] * (idx + 1.0))
    k_dec = jnp.exp(log_g[:, None] * (C - 1.0 - idx))
    chunk_dec = jnp.exp(log_g * C)
    bcast = lambda t: jnp.broadcast_to(t, (RET_HEADS, C, RET_HEAD_DIM))
    return (cos2, sin2, decay * qs,
            bcast(q_dec[:, :, None]) * qs, bcast(k_dec[:, :, None]), bcast(chunk_dec[:, None, None]))


def kernel(x, mem, w_in, conv_w, conv_b, conv_ln_g, conv_ln_b, ret_gn_g, ret_gn_b, w_out,
           ln1_g, ln1_b, w_xq, w_xk, w_xv, w_xo, ln2_g, ln2_b, w_up, w_down, ln3_g, ln3_b):
    assert w_in.shape[0] == DEPTH == 1
    assert RET_CHUNK == RET_HEAD_DIM
    tabs = _position_tables(x.shape[1])
    bf = lambda w: w[0].astype(BF16)
    k, v = _kv_proj(mem, bf(w_xk), bf(w_xv))
    x1 = _mixer(x, bf(w_in), conv_w[0], conv_b, conv_ln_g, conv_ln_b, ret_gn_g, ret_gn_b, bf(w_out),
                ln1_g, ln1_b, tabs)
    return _attn_mlp(x1, k, v, bf(w_xq), bf(w_xo), ln2_g, ln2_b, bf(w_up), bf(w_down), ln3_g, ln3_b)
```

```python
import functools

import jax
import jax.numpy as jnp
from jax import lax
from jax.experimental import pallas as pl
from jax.experimental.pallas import tpu as pltpu

D_MODEL = 1024
N_MEM = 256
CONV_WIDTH = 512
CONV_KERNEL = 31
RET_WIDTH = 512
RET_HEADS = 4
RET_HEAD_DIM = 128
RET_CHUNK = 128
ROPE_BASE = 10000.0
IN_COLS = 2 * CONV_WIDTH + 4 * RET_WIDTH
XATTN_HEADS = 4
XATTN_HEAD_DIM = 256
D_FF = 4 * D_MODEL
LN_EPS = 1e-5
DEPTH = 1
DN_ALPHA = (2.0 * DEPTH) ** 0.25

CONV_HALO = 32
SUBLANES, LANES = 8, 128
CONV_ROWS = 128
N_STRIPS = CONV_WIDTH // LANES
PROJ_COLS = 512
N_PIECES = IN_COLS // PROJ_COLS
P_A, P_B, P_Q, P_K, P_V, P_G = range(N_PIECES)
MIXER_TILE = 512
ATTN_TILE = 512
FF_CHUNK = 1024
VMEM_LIMIT_BYTES = 56 * 1024 * 1024
STAGE_BYTES = 1024 * 1024

BF16 = jnp.bfloat16
F32 = jnp.float32


def _dot(a, b):
    return jnp.dot(a, b, preferred_element_type=F32)


def _dot_nt(a, b):
    return lax.dot_general(a, b, (((1,), (1,)), ((), ())), preferred_element_type=F32)


def _sigmoid(x):
    return 1.0 / (1.0 + jnp.exp(-x))


def _layer_norm(x, g, b):
    mu = jnp.mean(x, axis=-1, keepdims=True)
    xc = x - mu
    var = jnp.mean(xc * xc, axis=-1, keepdims=True)
    return xc * lax.rsqrt(var + LN_EPS) * g + b


def _stage_shape(cols):
    rows = 1 << ((STAGE_BYTES // (4 * cols)).bit_length() - 1)
    return (2, rows, cols)


def _stage_weight_to_bf16(w_hbm_ref, w_bf16_ref, stage_ref, sem_ref):
    rows = stage_ref.shape[1]
    n_chunks = w_hbm_ref.shape[0] // rows
    assert n_chunks * rows == w_hbm_ref.shape[0] and stage_ref.shape[2] == w_hbm_ref.shape[1]

    def chunk_copy(i, slot):
        return pltpu.make_async_copy(w_hbm_ref.at[pl.ds(i * rows, rows)], stage_ref.at[slot], sem_ref.at[slot])

    chunk_copy(0, 0).start()

    def body(i, carry):
        slot = lax.rem(i, 2)

        @pl.when(i + 1 < n_chunks)
        def _():
            chunk_copy(i + 1, 1 - slot).start()

        chunk_copy(i, slot).wait()
        w_bf16_ref[pl.ds(pl.multiple_of(i * rows, rows), rows), :] = stage_ref[slot].astype(BF16)
        return carry

    lax.fori_loop(0, n_chunks, body, 0)


def _kv_kernel(mem_ref, wk_ref, wv_ref, k_ref, v_ref, wk_bf_ref, wv_bf_ref):
    @pl.when(pl.program_id(0) == 0)
    def _():
        wk_bf_ref[...] = wk_ref[...].astype(BF16)
        wv_bf_ref[...] = wv_ref[...].astype(BF16)

    m = mem_ref[0].astype(BF16)
    k_ref[0] = _dot(m, wk_bf_ref[...]).astype(BF16)
    v_ref[0] = _dot(m, wv_bf_ref[...]).astype(BF16)


def _kv_proj(mem, w_xk, w_xv):
    B = mem.shape[0]
    const = lambda b: (0, 0)
    return pl.pallas_call(
        _kv_kernel,
        grid=(B,),
        in_specs=[
            pl.BlockSpec((1, N_MEM, D_MODEL), lambda b: (b, 0, 0)),
            pl.BlockSpec((D_MODEL, D_MODEL), const),
            pl.BlockSpec((D_MODEL, D_MODEL), const),
        ],
        out_specs=[
            pl.BlockSpec((1, N_MEM, D_MODEL), lambda b: (b, 0, 0)),
            pl.BlockSpec((1, N_MEM, D_MODEL), lambda b: (b, 0, 0)),
        ],
        out_shape=[jax.ShapeDtypeStruct((B, N_MEM, D_MODEL), BF16)] * 2,
        scratch_shapes=[pltpu.VMEM((D_MODEL, D_MODEL), BF16)] * 2,
        compiler_params=pltpu.CompilerParams(dimension_semantics=("arbitrary",),
                                             vmem_limit_bytes=VMEM_LIMIT_BYTES),
        name="kv_proj",
    )(mem, w_xk, w_xv)


def _mixer_step(h_next_ref, h_ref, x_keep_ref, x_ref, seq_tile, w_in_ref, w_out_ref, x_next_ref, conv_w_ref,
                conv_b_ref, cln_g_ref, cln_b_ref, gn_g_ref, gn_b_ref, ln1_g_ref, ln1_b_ref, cos_ref, sin_ref,
                decay_ref, qdec_ref, kdec_ref, cdec_ref, o_ref, xb_ref, ubuf_ref, cbuf_ref, state_ref):
    T = MIXER_TILE
    x_next = x_next_ref[0]
    x_keep_ref[...] = x_next
    xb_ref[...] = x_next.astype(BF16)

    def project(p):
        h_next_ref[p] = _dot(xb_ref[...], w_in_ref[:, p * PROJ_COLS:(p + 1) * PROJ_COLS])

    u = h_ref[P_A] * _sigmoid(h_ref[P_B])
    def strip_rows(slab_parity, t_start, n):
        slab, parity = slab_parity
        return slab, pl.ds(2 * t_start + parity, n, stride=2), slice(None)

    for c in range(N_STRIPS):
        ubuf_ref[strip_rows(divmod(c, 2), CONV_HALO, T)] = u[:, c * LANES:(c + 1) * LANES]

    base = CONV_HALO - (CONV_KERNEL - 1)

    def conv_strip(c):
        where = divmod(c, 2)
        n_acc = CONV_ROWS // SUBLANES
        for t0 in range(0, T, CONV_ROWS):
            acc = [jnp.broadcast_to(conv_b_ref[c], (SUBLANES, LANES))] * n_acc
            for r in range(SUBLANES):
                taps = range(r, CONV_KERNEL, SUBLANES)
                wts = [jnp.broadcast_to(conv_w_ref[c, j:j + 1, :], (SUBLANES, LANES)) for j in taps]
                for m in range(n_acc + len(taps) - 1):
                    win = ubuf_ref[strip_rows(where, t0 + base + r + SUBLANES * m, SUBLANES)]
                    for a, wt in enumerate(wts):
                        if 0 <= m - a < n_acc:
                            acc[m - a] = acc[m - a] + win * wt
            cbuf_ref[c, t0:t0 + CONV_ROWS, :] = jnp.concatenate(acc, axis=0)

    for c in range(N_STRIPS):
        pl.when(pl.program_id(0) >= 0)(functools.partial(conv_strip, c))
    ubuf_ref[:, 0:2 * CONV_HALO, :] = ubuf_ref[:, 2 * T:2 * (T + CONV_HALO), :]
    project(0)
    conv = jnp.concatenate([cbuf_ref[c] for c in range(N_STRIPS)], axis=1)
    cn = _layer_norm(conv, cln_g_ref[...], cln_b_ref[...])
    conv_out = (cn * _sigmoid(cn)).astype(BF16)
    mix_conv = _dot(conv_out, w_out_ref[0:CONV_WIDTH, :])

    pos_rows = pl.ds(pl.multiple_of(seq_tile * T, T), T)
    cos, sin = cos_ref[pos_rows, :], sin_ref[pos_rows, :]
    half = RET_HEAD_DIM // 2
    C = RET_CHUNK
    n_chunks = T // C
    head_cols = [slice(h * RET_HEAD_DIM, (h + 1) * RET_HEAD_DIM) for h in range(RET_HEADS)]
    q, v, scores, kv = [], [], [], []
    project(1)
    for h, hs in enumerate(head_cols):
        qh = h_ref[P_Q, :, hs]
        kh = h_ref[P_K, :, hs]
        qh = qh * cos + pltpu.roll(qh, half, 1) * sin
        kh = kh * cos + pltpu.roll(kh, half, 1) * sin
        vh = h_ref[P_V, :, hs].astype(BF16)
        q.append(qh)
        v.append(vh)
        decay, kdec = decay_ref[h], kdec_ref[h]
        scores_h, kv_h = [], []
        for c in range(n_chunks):
            rows = slice(c * C, (c + 1) * C)
            scores_h.append((_dot_nt(qh[rows].astype(BF16), kh[rows].astype(BF16)) * decay).astype(BF16))
            kv_h.append(_dot((kh[rows] * kdec).T.astype(BF16), vh[rows]))
        scores.append(scores_h)
        kv.append(kv_h)
    project(2)
    ret_heads = []
    for h, hs in enumerate(head_cols):
        qdec, cdec = qdec_ref[h], cdec_ref[h]
        state = state_ref[h]
        ys = []
        for c in range(n_chunks):
            rows = slice(c * C, (c + 1) * C)
            lhs = jnp.concatenate([scores[h][c], (q[h][rows] * qdec).astype(BF16)], axis=1)
            rhs = jnp.concatenate([v[h][rows], state.astype(BF16)], axis=0)
            ys.append(_dot(lhs, rhs))
            state = state * cdec + kv[h][c]
        state_ref[h] = state
        y = jnp.concatenate(ys, axis=0)
        mu = jnp.mean(y, axis=-1, keepdims=True)
        yc = y - mu
        var = jnp.mean(yc * yc, axis=-1, keepdims=True)
        yn = yc * lax.rsqrt(var + LN_EPS)
        yn = yn * gn_g_ref[:, hs] + gn_b_ref[:, hs]
        gh = h_ref[P_G, :, hs]
        ret_heads.append((gh * _sigmoid(gh) * yn).astype(BF16))

    project(3)
    project(4)
    ret_out = jnp.concatenate(ret_heads, axis=1)
    mix = mix_conv + _dot(ret_out, w_out_ref[CONV_WIDTH:, :])
    project(5)
    o_ref[0] = _layer_norm(DN_ALPHA * x_ref[...] + mix, ln1_g_ref[...], ln1_b_ref[...])


def _mixer_kernel(tiles_per_seq, *refs):
    (w_in_hbm, w_out_hbm, *io_refs, h_even_ref, h_odd_ref, x_even_ref, x_odd_ref, xb_ref, ubuf_ref, cbuf_ref,
     state_ref, w_in_ref, w_out_ref, stage_in_ref, stage_out_ref, sem_ref) = refs
    n = pl.program_id(0)
    seq_tile = lax.rem(jnp.maximum(n - 1, 0), tiles_per_seq)

    @pl.when(n == 0)
    def _():
        _stage_weight_to_bf16(w_in_hbm, w_in_ref, stage_in_ref, sem_ref)
        _stage_weight_to_bf16(w_out_hbm, w_out_ref, stage_out_ref, sem_ref)
        h_odd_ref[...] = jnp.zeros(h_odd_ref.shape, F32)
        x_odd_ref[...] = jnp.zeros(x_odd_ref.shape, F32)

    @pl.when(seq_tile == 0)
    def _():
        ubuf_ref[:, 0:2 * CONV_HALO, :] = jnp.zeros((ubuf_ref.shape[0], 2 * CONV_HALO, LANES), F32)
        state_ref[...] = jnp.zeros(state_ref.shape, F32)

    def step(h_next_ref, h_ref, x_keep_ref, x_ref):
        _mixer_step(h_next_ref, h_ref, x_keep_ref, x_ref, seq_tile, w_in_ref, w_out_ref, *io_refs, xb_ref,
                    ubuf_ref, cbuf_ref, state_ref)

    @pl.when(lax.rem(n, 2) == 0)
    def _():
        step(h_even_ref, h_odd_ref, x_even_ref, x_odd_ref)

    @pl.when(lax.rem(n, 2) == 1)
    def _():
        step(h_odd_ref, h_even_ref, x_odd_ref, x_even_ref)


def _mixer(x, w_in, conv_w, conv_b, cln_g, cln_b, gn_g, gn_b, w_out, ln1_g, ln1_b, tabs):
    B, S, D = x.shape
    T = MIXER_TILE
    tiles_per_seq = S // T
    n_tiles = B * tiles_per_seq

    def tile_index(tile):
        return lax.div(tile, tiles_per_seq), lax.rem(tile, tiles_per_seq)

    def next_tile(n):
        b, s = tile_index(jnp.minimum(n, n_tiles - 1))
        return b, s, 0

    def this_tile(n):
        b, s = tile_index(jnp.maximum(n - 1, 0))
        return b, s, 0

    const2 = lambda n: (0, 0)
    const3 = lambda n: (0, 0, 0)
    row = lambda width: pl.BlockSpec((1, width), const2)
    pos = lambda: pl.BlockSpec((S, RET_HEAD_DIM), const2, pipeline_mode=pl.Buffered(1))
    tab = lambda: pl.BlockSpec((RET_HEADS, RET_CHUNK, RET_HEAD_DIM), const3)
    conv_w_s = conv_w.reshape(CONV_KERNEL, N_STRIPS, LANES).transpose(1, 0, 2)
    conv_b_s = conv_b.reshape(N_STRIPS, 1, LANES)
    return pl.pallas_call(
        functools.partial(_mixer_kernel, tiles_per_seq),
        grid=(n_tiles + 1,),
        in_specs=[
            pl.BlockSpec(memory_space=pl.ANY), pl.BlockSpec(memory_space=pl.ANY),
            pl.BlockSpec((1, T, D), next_tile),
            pl.BlockSpec((N_STRIPS, CONV_KERNEL, LANES), const3),
            pl.BlockSpec((N_STRIPS, 1, LANES), const3),
            row(CONV_WIDTH), row(CONV_WIDTH), row(RET_WIDTH), row(RET_WIDTH),
            row(D), row(D),
            pos(), pos(),
            tab(), tab(), tab(), tab(),
        ],
        out_specs=pl.BlockSpec((1, T, D), this_tile),
        out_shape=jax.ShapeDtypeStruct((B, S, D), F32),
        scratch_shapes=[
            pltpu.VMEM((N_PIECES, T, PROJ_COLS), F32),
            pltpu.VMEM((N_PIECES, T, PROJ_COLS), F32),
            pltpu.VMEM((T, D), F32),
            pltpu.VMEM((T, D), F32),
            pltpu.VMEM((T, D), BF16),
            pltpu.VMEM((N_STRIPS // 2, 2 * (CONV_HALO + T), LANES), F32),
            pltpu.VMEM((N_STRIPS, T, LANES), F32),
            pltpu.VMEM((RET_HEADS, RET_HEAD_DIM, RET_HEAD_DIM), F32),
            pltpu.VMEM((D, IN_COLS), BF16),
            pltpu.VMEM((D, D), BF16),
            pltpu.VMEM(_stage_shape(IN_COLS), F32),
            pltpu.VMEM(_stage_shape(D), F32),
            pltpu.SemaphoreType.DMA((2,)),
        ],
        compiler_params=pltpu.CompilerParams(dimension_semantics=("arbitrary",),
                                             vmem_limit_bytes=VMEM_LIMIT_BYTES),
        name="mixer",
    )(w_in, w_out, x, conv_w_s, conv_b_s, cln_g, cln_b, gn_g, gn_b, ln1_g, ln1_b, *tabs)


def _attn_mlp_kernel(x_ref, k_ref, v_ref, wq_hbm, wo_hbm, ln2_g_ref, ln2_b_ref, wup_hbm, wdn_hbm,
                     ln3_g_ref, ln3_b_ref, o_ref, wq_ref, wo_ref, wup_ref, wdn_ref, stage_d_ref, stage_ff_ref,
                     sem_ref):
    @pl.when((pl.program_id(0) == 0) & (pl.program_id(1) == 0))
    def _():
        _stage_weight_to_bf16(wq_hbm, wq_ref, stage_d_ref, sem_ref)
        _stage_weight_to_bf16(wo_hbm, wo_ref, stage_d_ref, sem_ref)
        _stage_weight_to_bf16(wup_hbm, wup_ref, stage_ff_ref, sem_ref)
        _stage_weight_to_bf16(wdn_hbm, wdn_ref, stage_d_ref, sem_ref)

    x = x_ref[0]
    T = x.shape[0]
    halves = (slice(0, T // 2), slice(T // 2, T))
    q = (_dot(x.astype(BF16), wq_ref[...]) * (XATTN_HEAD_DIM ** -0.5)).astype(BF16)
    head_cols = [slice(h * XATTN_HEAD_DIM, (h + 1) * XATTN_HEAD_DIM) for h in range(XATTN_HEADS)]
    scores = [_dot_nt(q[:, hs], k_ref[0, :, hs]) for hs in head_cols]
    heads = []
    for s, hs in zip(scores, head_cols):
        p = jnp.exp(s - jnp.max(s, axis=-1, keepdims=True))
        l = jnp.sum(p, axis=-1, keepdims=True)
        o = _dot(p.astype(BF16), v_ref[0, :, hs])
        heads.append((o * (1.0 / l)).astype(BF16))
    attn = jnp.concatenate(heads, axis=1)
    xa = [_dot(attn[rows], wo_ref[...]) for rows in halves]
    x2 = [_layer_norm(DN_ALPHA * x[rows] + xa_r, ln2_g_ref[...], ln2_b_ref[...]) for rows, xa_r in zip(halves, xa)]
    x2b_halves = [x2_r.astype(BF16) for x2_r in x2]
    n_chunks = D_FF // FF_CHUNK
    up_cols = [slice(j * FF_CHUNK, (j + 1) * FF_CHUNK) for j in range(n_chunks)]

    def act(up):
        r = jnp.maximum(up, 0.0)
        return (r * r).astype(BF16)

    up = jnp.concatenate([_dot(x2b_r, wup_ref[:, up_cols[0]]) for x2b_r in x2b_halves], axis=0)
    x2b = jnp.concatenate(x2b_halves, axis=0)
    ff = None
    for j in range(1, n_chunks):
        up_next = _dot(x2b, wup_ref[:, up_cols[j]])
        part = _dot(act(up), wdn_ref[up_cols[j - 1], :])
        ff = part if ff is None else ff + part
        up = up_next
    hid = act(up)
    for rows, x2_r in zip(halves, x2):
        ff_r = ff[rows] + _dot(hid[rows], wdn_ref[up_cols[-1], :])
        o_ref[0, rows, :] = _layer_norm(DN_ALPHA * x2_r + ff_r, ln3_g_ref[...], ln3_b_ref[...])


def _attn_mlp(x1, k, v, w_xq, w_xo, ln2_g, ln2_b, w_up, w_down, ln3_g, ln3_b):
    B, S, D = x1.shape
    T = ATTN_TILE
    const2 = lambda b, s: (0, 0)
    row = lambda n: pl.BlockSpec((1, n), const2)
    hbm = pl.BlockSpec(memory_space=pl.ANY)
    return pl.pallas_call(
        _attn_mlp_kernel,
        grid=(B, S // T),
        in_specs=[
            pl.BlockSpec((1, T, D), lambda b, s: (b, s, 0)),
            pl.BlockSpec((1, N_MEM, D), lambda b, s: (b, 0, 0)),
            pl.BlockSpec((1, N_MEM, D), lambda b, s: (b, 0, 0)),
            hbm, hbm,
            row(D), row(D),
            hbm, hbm,
            row(D), row(D),
        ],
        out_specs=pl.BlockSpec((1, T, D), lambda b, s: (b, s, 0)),
        out_shape=jax.ShapeDtypeStruct((B, S, D), F32),
        scratch_shapes=[
            pltpu.VMEM((D, D), BF16), pltpu.VMEM((D, D), BF16),
            pltpu.VMEM((D, D_FF), BF16), pltpu.VMEM((D_FF, D), BF16),
            pltpu.VMEM(_stage_shape(D), F32),
            pltpu.VMEM(_stage_shape(D_FF), F32),
            pltpu.SemaphoreType.DMA((2,)),
        ],
        compiler_params=pltpu.CompilerParams(dimension_semantics=("arbitrary", "arbitrary"),
                                             vmem_limit_bytes=VMEM_LIMIT_BYTES),
        name="attn_mlp",
    )(x1, k, v, w_xq, w_xo, ln2_g, ln2_b, w_up, w_down, ln3_g, ln3_b)


def _position_tables(seq):
    half = RET_HEAD_DIM // 2
    pos = jnp.arange(seq, dtype=F32)
    inv = ROPE_BASE ** (-jnp.linspace(0.0, 1.0, half, dtype=F32))
    ang = pos[:, None] * inv[None, :]
    cos, sin = jnp.cos(ang), jnp.sin(ang)
    cos2 = jnp.concatenate([cos, cos], axis=-1)
    sin2 = jnp.concatenate([-sin, sin], axis=-1)
    qs = RET_HEAD_DIM ** -0.5
    C = RET_CHUNK
    log_g = jnp.log(1.0 - 2.0 ** (-5.0 - jnp.arange(RET_HEADS, dtype=F32)))
    idx = jnp.arange(C, dtype=F32)
    rel = idx[:, None] - idx[None, :]
    decay = jnp.where(rel >= 0, jnp.exp(log_g[:, None, None] * jnp.maximum(rel, 0.0)), 0.0)
    q_dec = jnp.exp(log_g[:, None] * (idx + 1.0))
    k_dec = jnp.exp(log_g[:, None] * (C - 1.0 - idx))
    chunk_dec = jnp.exp(log_g * C)
    bcast = lambda t: jnp.broadcast_to(t, (RET_HEADS, C, RET_HEAD_DIM))
    return (cos2, sin2, decay * qs,
            bcast(q_dec[:, :, None]) * qs, bcast(k_dec[:, :, None]), bcast(chunk_dec[:, None, None]))


def kernel(x, mem, w_in, conv_w, conv_b, conv_ln_g, conv_ln_b, ret_gn_g, ret_gn_b, w_out,
           ln1_g, ln1_b, w_xq, w_xk, w_xv, w_xo, ln2_g, ln2_b, w_up, w_down, ln3_g, ln3_b):
    assert w_in.shape[0] == DEPTH == 1
    assert RET_CHUNK == RET_HEAD_DIM
    tabs = _position_tables(x.shape[1])
    k, v = _kv_proj(mem, w_xk[0], w_xv[0])
    x1 = _mixer(x, w_in[0], conv_w[0], conv_b, conv_ln_g, conv_ln_b, ret_gn_g, ret_gn_b, w_out[0],
                ln1_g, ln1_b, tabs)
    return _attn_mlp(x1, k, v, w_xq[0], w_xo[0], ln2_g, ln2_b, w_up[0], w_down[0], ln3_g, ln3_b)
```

```python
import functools

import jax
import jax.numpy as jnp
from jax import lax
from jax.experimental import pallas as pl
from jax.experimental.pallas import tpu as pltpu

D_MODEL = 1024
N_MEM = 256
CONV_WIDTH = 512
CONV_KERNEL = 31
RET_WIDTH = 512
RET_HEADS = 4
RET_HEAD_DIM = 128
RET_CHUNK = 128
ROPE_BASE = 10000.0
IN_COLS = 2 * CONV_WIDTH + 4 * RET_WIDTH
XATTN_HEADS = 4
XATTN_HEAD_DIM = 256
D_FF = 4 * D_MODEL
LN_EPS = 1e-5
DEPTH = 1
DN_ALPHA = (2.0 * DEPTH) ** 0.25

CONV_HALO = 32
SUBLANES, LANES = 8, 128
CONV_ROWS = 128
N_STRIPS = CONV_WIDTH // LANES
PROJ_COLS = 512
N_PIECES = IN_COLS // PROJ_COLS
P_A, P_B, P_Q, P_K, P_V, P_G = range(N_PIECES)
MIXER_TILE = 512
ATTN_TILE = 512
FF_CHUNK = 1024
TAIL_SPLIT = 4
VMEM_LIMIT_BYTES = 56 * 1024 * 1024
STAGE_BYTES = 1024 * 1024

BF16 = jnp.bfloat16
F32 = jnp.float32


def _dot(a, b):
    return jnp.dot(a, b, preferred_element_type=F32)


def _dot_nt(a, b):
    return lax.dot_general(a, b, (((1,), (1,)), ((), ())), preferred_element_type=F32)


def _sigmoid(x):
    return 1.0 / (1.0 + jnp.exp(-x))


def _layer_norm(x, g, b):
    mu = jnp.mean(x, axis=-1, keepdims=True)
    xc = x - mu
    var = jnp.mean(xc * xc, axis=-1, keepdims=True)
    return xc * lax.rsqrt(var + LN_EPS) * g + b


def _stage_shape(cols):
    rows = 1 << ((STAGE_BYTES // (4 * cols)).bit_length() - 1)
    return (2, rows, cols)


def _stage_weight_to_bf16(w_hbm_ref, w_bf16_ref, stage_ref, sem_ref):
    rows = stage_ref.shape[1]
    n_chunks = w_hbm_ref.shape[0] // rows
    assert n_chunks * rows == w_hbm_ref.shape[0] and stage_ref.shape[2] == w_hbm_ref.shape[1]

    def chunk_copy(i, slot):
        return pltpu.make_async_copy(w_hbm_ref.at[pl.ds(i * rows, rows)], stage_ref.at[slot], sem_ref.at[slot])

    chunk_copy(0, 0).start()

    def body(i, carry):
        slot = lax.rem(i, 2)

        @pl.when(i + 1 < n_chunks)
        def _():
            chunk_copy(i + 1, 1 - slot).start()

        chunk_copy(i, slot).wait()
        w_bf16_ref[pl.ds(pl.multiple_of(i * rows, rows), rows), :] = stage_ref[slot].astype(BF16)
        return carry

    lax.fori_loop(0, n_chunks, body, 0)


def _kv_kernel(mem_ref, wk_ref, wv_ref, k_ref, v_ref, wk_bf_ref, wv_bf_ref):
    @pl.when(pl.program_id(0) == 0)
    def _():
        wk_bf_ref[...] = wk_ref[...].astype(BF16)
        wv_bf_ref[...] = wv_ref[...].astype(BF16)

    m = mem_ref[0].astype(BF16)
    k_ref[0] = _dot(m, wk_bf_ref[...]).astype(BF16)
    v_ref[0] = _dot(m, wv_bf_ref[...]).astype(BF16)


def _kv_proj(mem, w_xk, w_xv):
    B = mem.shape[0]
    const = lambda b: (0, 0)
    return pl.pallas_call(
        _kv_kernel,
        grid=(B,),
        in_specs=[
            pl.BlockSpec((1, N_MEM, D_MODEL), lambda b: (b, 0, 0)),
            pl.BlockSpec((D_MODEL, D_MODEL), const),
            pl.BlockSpec((D_MODEL, D_MODEL), const),
        ],
        out_specs=[
            pl.BlockSpec((1, N_MEM, D_MODEL), lambda b: (b, 0, 0)),
            pl.BlockSpec((1, N_MEM, D_MODEL), lambda b: (b, 0, 0)),
        ],
        out_shape=[jax.ShapeDtypeStruct((B, N_MEM, D_MODEL), BF16)] * 2,
        scratch_shapes=[pltpu.VMEM((D_MODEL, D_MODEL), BF16)] * 2,
        compiler_params=pltpu.CompilerParams(dimension_semantics=("arbitrary",),
                                             vmem_limit_bytes=VMEM_LIMIT_BYTES),
        name="kv_proj",
    )(mem, w_xk, w_xv)


def _mixer_step(h_next_ref, h_ref, x_keep_ref, x_ref, seq_tile, w_in_ref, w_out_ref, x_next_ref, conv_w_ref,
                conv_b_ref, cln_g_ref, cln_b_ref, gn_g_ref, gn_b_ref, ln1_g_ref, ln1_b_ref, cos_ref, sin_ref,
                decay_ref, qdec_ref, kdec_ref, cdec_ref, o_ref, xb_ref, ubuf_ref, cbuf_ref, state_ref):
    T = MIXER_TILE

    def project(p):
        h_next_ref[p] = _dot(xb_ref[...], w_in_ref[:, p * PROJ_COLS:(p + 1) * PROJ_COLS])

    def strip_rows(slab_parity, t_start, n):
        slab, parity = slab_parity
        return slab, pl.ds(2 * t_start + parity, n, stride=2), slice(None)

    base = CONV_HALO - (CONV_KERNEL - 1)

    def conv_strip(c):
        where = divmod(c, 2)
        n_acc = CONV_ROWS // SUBLANES
        for t0 in range(0, T, CONV_ROWS):
            acc = [jnp.broadcast_to(conv_b_ref[c], (SUBLANES, LANES))] * n_acc
            for r in range(SUBLANES):
                taps = range(r, CONV_KERNEL, SUBLANES)
                wts = [jnp.broadcast_to(conv_w_ref[c, j:j + 1, :], (SUBLANES, LANES)) for j in taps]
                for m in range(n_acc + len(taps) - 1):
                    win = ubuf_ref[strip_rows(where, t0 + base + r + SUBLANES * m, SUBLANES)]
                    for a, wt in enumerate(wts):
                        if 0 <= m - a < n_acc:
                            acc[m - a] = acc[m - a] + win * wt
            cbuf_ref[c, t0:t0 + CONV_ROWS, :] = jnp.concatenate(acc, axis=0)

    for c in range(N_STRIPS):
        pl.when(pl.program_id(0) >= 0)(functools.partial(conv_strip, c))
    ubuf_ref[:, 0:2 * CONV_HALO, :] = ubuf_ref[:, 2 * T:2 * (T + CONV_HALO), :]
    x_next = x_next_ref[0]
    xb_ref[...] = x_next.astype(BF16)
    project(P_A)
    x_keep_ref[...] = x_next
    project(P_B)
    u_next = h_next_ref[P_A] * _sigmoid(h_next_ref[P_B])
    for c in range(N_STRIPS):
        ubuf_ref[strip_rows(divmod(c, 2), CONV_HALO, T)] = u_next[:, c * LANES:(c + 1) * LANES]
    conv = jnp.concatenate([cbuf_ref[c] for c in range(N_STRIPS)], axis=1)
    cn = _layer_norm(conv, cln_g_ref[...], cln_b_ref[...])
    conv_out = (cn * _sigmoid(cn)).astype(BF16)
    mix_conv = _dot(conv_out, w_out_ref[0:CONV_WIDTH, :])

    pos_rows = pl.ds(pl.multiple_of(seq_tile * T, T), T)
    cos, sin = cos_ref[pos_rows, :], sin_ref[pos_rows, :]
    half = RET_HEAD_DIM // 2
    C = RET_CHUNK
    n_chunks = T // C
    head_cols = [slice(h * RET_HEAD_DIM, (h + 1) * RET_HEAD_DIM) for h in range(RET_HEADS)]
    q, v, scores, kv = [], [], [], []
    project(P_Q)
    for h, hs in enumerate(head_cols):
        qh = h_ref[P_Q, :, hs]
        kh = h_ref[P_K, :, hs]
        qh = qh * cos + pltpu.roll(qh, half, 1) * sin
        kh = kh * cos + pltpu.roll(kh, half, 1) * sin
        vh = h_ref[P_V, :, hs].astype(BF16)
        q.append(qh)
        v.append(vh)
        decay, kdec = decay_ref[h], kdec_ref[h]
        scores_h, kv_h = [], []
        for c in range(n_chunks):
            rows = slice(c * C, (c + 1) * C)
            scores_h.append((_dot_nt(qh[rows].astype(BF16), kh[rows].astype(BF16)) * decay).astype(BF16))
            kv_h.append(_dot((kh[rows] * kdec).T.astype(BF16), vh[rows]))
        scores.append(scores_h)
        kv.append(kv_h)
    project(P_K)
    ret_heads = []
    for h, hs in enumerate(head_cols):
        qdec, cdec = qdec_ref[h], cdec_ref[h]
        state = state_ref[h]
        ys = []
        for c in range(n_chunks):
            rows = slice(c * C, (c + 1) * C)
            lhs = jnp.concatenate([scores[h][c], (q[h][rows] * qdec).astype(BF16)], axis=1)
            rhs = jnp.concatenate([v[h][rows], state.astype(BF16)], axis=0)
            ys.append(_dot(lhs, rhs))
            state = state * cdec + kv[h][c]
        state_ref[h] = state
        y = jnp.concatenate(ys, axis=0)
        mu = jnp.mean(y, axis=-1, keepdims=True)
        yc = y - mu
        var = jnp.mean(yc * yc, axis=-1, keepdims=True)
        yn = yc * lax.rsqrt(var + LN_EPS)
        yn = yn * gn_g_ref[:, hs] + gn_b_ref[:, hs]
        gh = h_ref[P_G, :, hs]
        ret_heads.append((gh * _sigmoid(gh) * yn).astype(BF16))

    project(P_V)
    ret_out = jnp.concatenate(ret_heads, axis=1)
    mix = [mix_conv[r0:r0 + T // TAIL_SPLIT] + _dot(ret_out[r0:r0 + T // TAIL_SPLIT], w_out_ref[CONV_WIDTH:, :])
           for r0 in range(0, T, T // TAIL_SPLIT)]
    project(P_G)
    for i, mix_r in enumerate(mix):
        rows = slice(i * (T // TAIL_SPLIT), (i + 1) * (T // TAIL_SPLIT))
        o_ref[0, rows, :] = _layer_norm(DN_ALPHA * x_ref[rows, :] + mix_r, ln1_g_ref[...], ln1_b_ref[...])


def _mixer_kernel(tiles_per_seq, *refs):
    (w_in_hbm, w_out_hbm, *io_refs, h_even_ref, h_odd_ref, x_even_ref, x_odd_ref, xb_ref, ubuf_ref, cbuf_ref,
     state_ref, w_in_ref, w_out_ref, stage_in_ref, stage_out_ref, sem_ref) = refs
    n = pl.program_id(0)
    seq_tile = lax.rem(jnp.maximum(n - 1, 0), tiles_per_seq)

    @pl.when(n == 0)
    def _():
        _stage_weight_to_bf16(w_in_hbm, w_in_ref, stage_in_ref, sem_ref)
        _stage_weight_to_bf16(w_out_hbm, w_out_ref, stage_out_ref, sem_ref)
        h_odd_ref[...] = jnp.zeros(h_odd_ref.shape, F32)
        x_odd_ref[...] = jnp.zeros(x_odd_ref.shape, F32)
        ubuf_ref[...] = jnp.zeros(ubuf_ref.shape, F32)

    @pl.when(seq_tile == 0)
    def _():
        ubuf_ref[:, 0:2 * CONV_HALO, :] = jnp.zeros((ubuf_ref.shape[0], 2 * CONV_HALO, LANES), F32)
        state_ref[...] = jnp.zeros(state_ref.shape, F32)

    def step(h_next_ref, h_ref, x_keep_ref, x_ref):
        _mixer_step(h_next_ref, h_ref, x_keep_ref, x_ref, seq_tile, w_in_ref, w_out_ref, *io_refs, xb_ref,
                    ubuf_ref, cbuf_ref, state_ref)

    @pl.when(lax.rem(n, 2) == 0)
    def _():
        step(h_even_ref, h_odd_ref, x_even_ref, x_odd_ref)

    @pl.when(lax.rem(n, 2) == 1)
    def _():
        step(h_odd_ref, h_even_ref, x_odd_ref, x_even_ref)


def _mixer(x, w_in, conv_w, conv_b, cln_g, cln_b, gn_g, gn_b, w_out, ln1_g, ln1_b, tabs):
    B, S, D = x.shape
    T = MIXER_TILE
    tiles_per_seq = S // T
    n_tiles = B * tiles_per_seq

    def tile_index(tile):
        return lax.div(tile, tiles_per_seq), lax.rem(tile, tiles_per_seq)

    def next_tile(n):
        b, s = tile_index(jnp.minimum(n, n_tiles - 1))
        return b, s, 0

    def this_tile(n):
        b, s = tile_index(jnp.maximum(n - 1, 0))
        return b, s, 0

    const2 = lambda n: (0, 0)
    const3 = lambda n: (0, 0, 0)
    row = lambda width: pl.BlockSpec((1, width), const2)
    pos = lambda: pl.BlockSpec((S, RET_HEAD_DIM), const2, pipeline_mode=pl.Buffered(1))
    tab = lambda: pl.BlockSpec((RET_HEADS, RET_CHUNK, RET_HEAD_DIM), const3)
    conv_w_s = conv_w.reshape(CONV_KERNEL, N_STRIPS, LANES).transpose(1, 0, 2)
    conv_b_s = conv_b.reshape(N_STRIPS, 1, LANES)
    return pl.pallas_call(
        functools.partial(_mixer_kernel, tiles_per_seq),
        grid=(n_tiles + 1,),
        in_specs=[
            pl.BlockSpec(memory_space=pl.ANY), pl.BlockSpec(memory_space=pl.ANY),
            pl.BlockSpec((1, T, D), next_tile),
            pl.BlockSpec((N_STRIPS, CONV_KERNEL, LANES), const3),
            pl.BlockSpec((N_STRIPS, 1, LANES), const3),
            row(CONV_WIDTH), row(CONV_WIDTH), row(RET_WIDTH), row(RET_WIDTH),
            row(D), row(D),
            pos(), pos(),
            tab(), tab(), tab(), tab(),
        ],
        out_specs=pl.BlockSpec((1, T, D), this_tile),
        out_shape=jax.ShapeDtypeStruct((B, S, D), F32),
        scratch_shapes=[
            pltpu.VMEM((N_PIECES, T, PROJ_COLS), F32),
            pltpu.VMEM((N_PIECES, T, PROJ_COLS), F32),
            pltpu.VMEM((T, D), F32),
            pltpu.VMEM((T, D), F32),
            pltpu.VMEM((T, D), BF16),
            pltpu.VMEM((N_STRIPS // 2, 2 * (CONV_HALO + T), LANES), F32),
            pltpu.VMEM((N_STRIPS, T, LANES), F32),
            pltpu.VMEM((RET_HEADS, RET_HEAD_DIM, RET_HEAD_DIM), F32),
            pltpu.VMEM((D, IN_COLS), BF16),
            pltpu.VMEM((D, D), BF16),
            pltpu.VMEM(_stage_shape(IN_COLS), F32),
            pltpu.VMEM(_stage_shape(D), F32),
            pltpu.SemaphoreType.DMA((2,)),
        ],
        compiler_params=pltpu.CompilerParams(dimension_semantics=("arbitrary",),
                                             vmem_limit_bytes=VMEM_LIMIT_BYTES),
        name="mixer",
    )(w_in, w_out, x, conv_w_s, conv_b_s, cln_g, cln_b, gn_g, gn_b, ln1_g, ln1_b, *tabs)


def _attn_mlp_kernel(x_ref, k_ref, v_ref, wq_hbm, wo_hbm, ln2_g_ref, ln2_b_ref, wup_hbm, wdn_hbm,
                     ln3_g_ref, ln3_b_ref, o_ref, wq_ref, wo_ref, wup_ref, wdn_ref, stage_d_ref, stage_ff_ref,
                     sem_ref):
    @pl.when((pl.program_id(0) == 0) & (pl.program_id(1) == 0))
    def _():
        _stage_weight_to_bf16(wq_hbm, wq_ref, stage_d_ref, sem_ref)
        _stage_weight_to_bf16(wo_hbm, wo_ref, stage_d_ref, sem_ref)
        _stage_weight_to_bf16(wup_hbm, wup_ref, stage_ff_ref, sem_ref)
        _stage_weight_to_bf16(wdn_hbm, wdn_ref, stage_d_ref, sem_ref)

    x = x_ref[0]
    T = x.shape[0]
    halves = (slice(0, T // 2), slice(T // 2, T))
    q = (_dot(x.astype(BF16), wq_ref[...]) * (XATTN_HEAD_DIM ** -0.5)).astype(BF16)
    head_cols = [slice(h * XATTN_HEAD_DIM, (h + 1) * XATTN_HEAD_DIM) for h in range(XATTN_HEADS)]
    scores = [_dot_nt(q[:, hs], k_ref[0, :, hs]) for hs in head_cols]
    heads = []
    for s, hs in zip(scores, head_cols):
        p = jnp.exp(s - jnp.max(s, axis=-1, keepdims=True))
        l = jnp.sum(p, axis=-1, keepdims=True)
        o = _dot(p.astype(BF16), v_ref[0, :, hs])
        heads.append((o * (1.0 / l)).astype(BF16))
    attn = jnp.concatenate(heads, axis=1)
    xa = [_dot(attn[rows], wo_ref[...]) for rows in halves]
    x2 = [_layer_norm(DN_ALPHA * x[rows] + xa_r, ln2_g_ref[...], ln2_b_ref[...]) for rows, xa_r in zip(halves, xa)]
    x2b_halves = [x2_r.astype(BF16) for x2_r in x2]
    n_chunks = D_FF // FF_CHUNK
    up_cols = [slice(j * FF_CHUNK, (j + 1) * FF_CHUNK) for j in range(n_chunks)]

    def act(up):
        r = jnp.maximum(up, 0.0)
        return (r * r).astype(BF16)

    up = jnp.concatenate([_dot(x2b_r, wup_ref[:, up_cols[0]]) for x2b_r in x2b_halves], axis=0)
    x2b = jnp.concatenate(x2b_halves, axis=0)
    ff = None
    for j in range(1, n_chunks):
        up_next = _dot(x2b, wup_ref[:, up_cols[j]])
        part = _dot(act(up), wdn_ref[up_cols[j - 1], :])
        ff = part if ff is None else ff + part
        up = up_next
    hid = act(up)
    x2 = jnp.concatenate(x2, axis=0)
    for r0 in range(0, T, T // TAIL_SPLIT):
        rows = slice(r0, r0 + T // TAIL_SPLIT)
        ff_r = ff[rows] + _dot(hid[rows], wdn_ref[up_cols[-1], :])
        o_ref[0, rows, :] = _layer_norm(DN_ALPHA * x2[rows] + ff_r, ln3_g_ref[...], ln3_b_ref[...])


def _attn_mlp(x1, k, v, w_xq, w_xo, ln2_g, ln2_b, w_up, w_down, ln3_g, ln3_b):
    B, S, D = x1.shape
    T = ATTN_TILE
    const2 = lambda b, s: (0, 0)
    row = lambda n: pl.BlockSpec((1, n), const2)
    hbm = pl.BlockSpec(memory_space=pl.ANY)
    return pl.pallas_call(
        _attn_mlp_kernel,
        grid=(B, S // T),
        in_specs=[
            pl.BlockSpec((1, T, D), lambda b, s: (b, s, 0)),
            pl.BlockSpec((1, N_MEM, D), lambda b, s: (b, 0, 0)),
            pl.BlockSpec((1, N_MEM, D), lambda b, s: (b, 0, 0)),
            hbm, hbm,
            row(D), row(D),
            hbm, hbm,
            row(D), row(D),
        ],
        out_specs=pl.BlockSpec((1, T, D), lambda b, s: (b, s, 0)),
        out_shape=jax.ShapeDtypeStruct((B, S, D), F32),
        scratch_shapes=[
            pltpu.VMEM((D, D), BF16), pltpu.VMEM((D, D), BF16),
            pltpu.VMEM((D, D_FF), BF16), pltpu.VMEM((D_FF, D), BF16),
            pltpu.VMEM(_stage_shape(D), F32),
            pltpu.VMEM(_stage_shape(D_FF), F32),
            pltpu.SemaphoreType.DMA((2,)),
        ],
        compiler_params=pltpu.CompilerParams(dimension_semantics=("arbitrary", "arbitrary"),
                                             vmem_limit_bytes=VMEM_LIMIT_BYTES),
        name="attn_mlp",
    )(x1, k, v, w_xq, w_xo, ln2_g, ln2_b, w_up, w_down, ln3_g, ln3_b)


def _position_tables(seq):
    half = RET_HEAD_DIM // 2
    pos = jnp.arange(seq, dtype=F32)
    inv = ROPE_BASE ** (-jnp.linspace(0.0, 1.0, half, dtype=F32))
    ang = pos[:, None] * inv[None, :]
    cos, sin = jnp.cos(ang), jnp.sin(ang)
    cos2 = jnp.concatenate([cos, cos], axis=-1)
    sin2 = jnp.concatenate([-sin, sin], axis=-1)
    qs = RET_HEAD_DIM ** -0.5
    C = RET_CHUNK
    log_g = jnp.log(1.0 - 2.0 ** (-5.0 - jnp.arange(RET_HEADS, dtype=F32)))
    idx = jnp.arange(C, dtype=F32)
    rel = idx[:, None] - idx[None, :]
    decay = jnp.where(rel >= 0, jnp.exp(log_g[:, None, None] * jnp.maximum(rel, 0.0)), 0.0)
    q_dec = jnp.exp(log_g[:, None] * (idx + 1.0))
    k_dec = jnp.exp(log_g[:, None] * (C - 1.0 - idx))
    chunk_dec = jnp.exp(log_g * C)
    bcast = lambda t: jnp.broadcast_to(t, (RET_HEADS, C, RET_HEAD_DIM))
    return (cos2, sin2, decay * qs,
            bcast(q_dec[:, :, None]) * qs, bcast(k_dec[:, :, None]), bcast(chunk_dec[:, None, None]))


def kernel(x, mem, w_in, conv_w, conv_b, conv_ln_g, conv_ln_b, ret_gn_g, ret_gn_b, w_out,
           ln1_g, ln1_b, w_xq, w_xk, w_xv, w_xo, ln2_g, ln2_b, w_up, w_down, ln3_g, ln3_b):
    assert w_in.shape[0] == DEPTH == 1
    assert RET_CHUNK == RET_HEAD_DIM
    tabs = _position_tables(x.shape[1])
    k, v = _kv_proj(mem, w_xk[0], w_xv[0])
    x1 = _mixer(x, w_in[0], conv_w[0], conv_b, conv_ln_g, conv_ln_b, ret_gn_g, ret_gn_b, w_out[0],
                ln1_g, ln1_b, tabs)
    return _attn_mlp(x1, k, v, w_xq[0], w_xo[0], ln2_g, ln2_b, w_up[0], w_down[0], ln3_g, ln3_b)
```

```python
import functools

import jax
import jax.numpy as jnp
from jax import lax
from jax.experimental import pallas as pl
from jax.experimental.pallas import tpu as pltpu

D_MODEL = 1024
N_MEM = 256
CONV_WIDTH = 512
CONV_KERNEL = 31
RET_WIDTH = 512
RET_HEADS = 4
RET_HEAD_DIM = 128
RET_CHUNK = 128
ROPE_BASE = 10000.0
IN_COLS = 2 * CONV_WIDTH + 4 * RET_WIDTH
XATTN_HEADS = 4
XATTN_HEAD_DIM = 256
D_FF = 4 * D_MODEL
LN_EPS = 1e-5
DEPTH = 1
DN_ALPHA = (2.0 * DEPTH) ** 0.25

CONV_HALO = 32
SUBLANES, LANES = 8, 128
CONV_ROWS = 128
N_STRIPS = CONV_WIDTH // LANES
PROJ_COLS = 512
N_PIECES = IN_COLS // PROJ_COLS
P_A, P_B, P_Q, P_K, P_V, P_G = range(N_PIECES)
MIXER_TILE = 512
ATTN_TILE = 512
FF_CHUNK = 1024
TAIL_SPLIT = 4
VMEM_LIMIT_BYTES = 56 * 1024 * 1024
STAGE_BYTES = 1024 * 1024

BF16 = jnp.bfloat16
F32 = jnp.float32


def _dot(a, b):
    return jnp.dot(a, b, preferred_element_type=F32)


def _dot_nt(a, b):
    return lax.dot_general(a, b, (((1,), (1,)), ((), ())), preferred_element_type=F32)


def _sigmoid(x):
    return 1.0 / (1.0 + jnp.exp(-x))


def _layer_norm(x, g, b):
    mu = jnp.mean(x, axis=-1, keepdims=True)
    xc = x - mu
    var = jnp.mean(xc * xc, axis=-1, keepdims=True)
    return xc * lax.rsqrt(var + LN_EPS) * g + b


def _stage_shape(cols):
    rows = 1 << ((STAGE_BYTES // (4 * cols)).bit_length() - 1)
    return (2, rows, cols)


def _stage_weight_to_bf16(w_hbm_ref, w_bf16_ref, stage_ref, sem_ref):
    rows = stage_ref.shape[1]
    n_chunks = w_hbm_ref.shape[0] // rows
    assert n_chunks * rows == w_hbm_ref.shape[0] and stage_ref.shape[2] == w_hbm_ref.shape[1]

    def chunk_copy(i, slot):
        return pltpu.make_async_copy(w_hbm_ref.at[pl.ds(i * rows, rows)], stage_ref.at[slot], sem_ref.at[slot])

    chunk_copy(0, 0).start()

    def body(i, carry):
        slot = lax.rem(i, 2)

        @pl.when(i + 1 < n_chunks)
        def _():
            chunk_copy(i + 1, 1 - slot).start()

        chunk_copy(i, slot).wait()
        w_bf16_ref[pl.ds(pl.multiple_of(i * rows, rows), rows), :] = stage_ref[slot].astype(BF16)
        return carry

    lax.fori_loop(0, n_chunks, body, 0)


def _kv_kernel(mem_ref, wk_ref, wv_ref, k_ref, v_ref, wk_bf_ref, wv_bf_ref):
    @pl.when(pl.program_id(0) == 0)
    def _():
        wk_bf_ref[...] = wk_ref[...].astype(BF16)
        wv_bf_ref[...] = wv_ref[...].astype(BF16)

    m = mem_ref[0].astype(BF16)
    k_ref[0] = _dot(m, wk_bf_ref[...]).astype(BF16)
    v_ref[0] = _dot(m, wv_bf_ref[...]).astype(BF16)


def _kv_proj(mem, w_xk, w_xv):
    B = mem.shape[0]
    const = lambda b: (0, 0)
    return pl.pallas_call(
        _kv_kernel,
        grid=(B,),
        in_specs=[
            pl.BlockSpec((1, N_MEM, D_MODEL), lambda b: (b, 0, 0)),
            pl.BlockSpec((D_MODEL, D_MODEL), const),
            pl.BlockSpec((D_MODEL, D_MODEL), const),
        ],
        out_specs=[
            pl.BlockSpec((1, N_MEM, D_MODEL), lambda b: (b, 0, 0)),
            pl.BlockSpec((1, N_MEM, D_MODEL), lambda b: (b, 0, 0)),
        ],
        out_shape=[jax.ShapeDtypeStruct((B, N_MEM, D_MODEL), BF16)] * 2,
        scratch_shapes=[pltpu.VMEM((D_MODEL, D_MODEL), BF16)] * 2,
        compiler_params=pltpu.CompilerParams(dimension_semantics=("arbitrary",),
                                             vmem_limit_bytes=VMEM_LIMIT_BYTES),
        name="kv_proj",
    )(mem, w_xk, w_xv)


def _mixer_step(h_next_ref, h_ref, x_keep_ref, x_ref, seq_tile, w_in_ref, w_out_ref, x_next_ref, conv_w_ref,
                conv_b_ref, cln_g_ref, cln_b_ref, gn_g_ref, gn_b_ref, ln1_g_ref, ln1_b_ref, cos_ref, sin_ref,
                decay_ref, qdec_ref, kdec_ref, cdec_ref, o_ref, xb_ref, ubuf_ref, cbuf_ref, state_ref):
    T = MIXER_TILE

    def project(p):
        h_next_ref[p] = _dot(xb_ref[...], w_in_ref[:, p * PROJ_COLS:(p + 1) * PROJ_COLS])

    def strip_rows(slab_parity, t_start, n):
        slab, parity = slab_parity
        return slab, pl.ds(2 * t_start + parity, n, stride=2), slice(None)

    base = CONV_HALO - (CONV_KERNEL - 1)

    def conv_strip(c):
        where = divmod(c, 2)
        n_acc = CONV_ROWS // SUBLANES
        for t0 in range(0, T, CONV_ROWS):
            acc = [jnp.broadcast_to(conv_b_ref[c], (SUBLANES, LANES))] * n_acc
            for r in range(SUBLANES):
                taps = range(r, CONV_KERNEL, SUBLANES)
                wts = [jnp.broadcast_to(conv_w_ref[c, j:j + 1, :], (SUBLANES, LANES)) for j in taps]
                for m in range(n_acc + len(taps) - 1):
                    win = ubuf_ref[strip_rows(where, t0 + base + r + SUBLANES * m, SUBLANES)]
                    for a, wt in enumerate(wts):
                        if 0 <= m - a < n_acc:
                            acc[m - a] = acc[m - a] + win * wt
            cbuf_ref[c, t0:t0 + CONV_ROWS, :] = jnp.concatenate(acc, axis=0)

    for c in range(N_STRIPS):
        pl.when(pl.program_id(0) >= 0)(functools.partial(conv_strip, c))
    ubuf_ref[:, 0:2 * CONV_HALO, :] = ubuf_ref[:, 2 * T:2 * (T + CONV_HALO), :]
    x_next = x_next_ref[0]
    xb_ref[...] = x_next.astype(BF16)
    project(P_A)
    x_keep_ref[...] = x_next
    project(P_B)
    row_groups = [slice(r0, r0 + T // TAIL_SPLIT) for r0 in range(0, T, T // TAIL_SPLIT)]
    mix_conv = []
    for rows in row_groups:
        conv = jnp.concatenate([cbuf_ref[c, rows, :] for c in range(N_STRIPS)], axis=1)
        cn = _layer_norm(conv, cln_g_ref[...], cln_b_ref[...])
        conv_out = (cn * _sigmoid(cn)).astype(BF16)
        mix_conv.append(_dot(conv_out, w_out_ref[0:CONV_WIDTH, :]))
    project(P_Q)
    u_next = h_next_ref[P_A] * _sigmoid(h_next_ref[P_B])
    for c in range(N_STRIPS):
        ubuf_ref[strip_rows(divmod(c, 2), CONV_HALO, T)] = u_next[:, c * LANES:(c + 1) * LANES]

    pos_rows = pl.ds(pl.multiple_of(seq_tile * T, T), T)
    cos, sin = cos_ref[pos_rows, :], sin_ref[pos_rows, :]
    half = RET_HEAD_DIM // 2
    C = RET_CHUNK
    n_chunks = T // C
    head_cols = [slice(h * RET_HEAD_DIM, (h + 1) * RET_HEAD_DIM) for h in range(RET_HEADS)]
    chunk_rows = [slice(c * C, (c + 1) * C) for c in range(n_chunks)]
    q, k, v, scores, kv = [], [], [], [], []
    for h, hs in enumerate(head_cols):
        qh = h_ref[P_Q, :, hs]
        kh = h_ref[P_K, :, hs]
        qh = qh * cos + pltpu.roll(qh, half, 1) * sin
        kh = kh * cos + pltpu.roll(kh, half, 1) * sin
        q.append(qh)
        k.append(kh)
        v.append(h_ref[P_V, :, hs].astype(BF16))
        scores.append([(_dot_nt(qh[rows].astype(BF16), kh[rows].astype(BF16)) * decay_ref[h]).astype(BF16)
                       for rows in chunk_rows])
    for h in range(RET_HEADS):
        kv.append([_dot((k[h][rows] * kdec_ref[h]).T.astype(BF16), v[h][rows]) for rows in chunk_rows])
    project(P_K)
    states = [state_ref[h] for h in range(RET_HEADS)]
    ys = [[] for _ in range(RET_HEADS)]
    for c, rows in enumerate(chunk_rows):
        for h in range(RET_HEADS):
            lhs = jnp.concatenate([scores[h][c], (q[h][rows] * qdec_ref[h]).astype(BF16)], axis=1)
            rhs = jnp.concatenate([v[h][rows], states[h].astype(BF16)], axis=0)
            ys[h].append(_dot(lhs, rhs))
            states[h] = states[h] * cdec_ref[h] + kv[h][c]
    ret_heads = []
    for h, hs in enumerate(head_cols):
        state_ref[h] = states[h]
        y = jnp.concatenate(ys[h], axis=0)
        mu = jnp.mean(y, axis=-1, keepdims=True)
        yc = y - mu
        var = jnp.mean(yc * yc, axis=-1, keepdims=True)
        yn = yc * lax.rsqrt(var + LN_EPS)
        yn = yn * gn_g_ref[:, hs] + gn_b_ref[:, hs]
        gh = h_ref[P_G, :, hs]
        ret_heads.append((gh * _sigmoid(gh) * yn).astype(BF16))

    project(P_V)
    ret_out = jnp.concatenate(ret_heads, axis=1)
    mix = [mix_conv_r + _dot(ret_out[rows], w_out_ref[CONV_WIDTH:, :])
           for rows, mix_conv_r in zip(row_groups, mix_conv)]
    project(P_G)
    for rows, mix_r in zip(row_groups, mix):
        o_ref[0, rows, :] = _layer_norm(DN_ALPHA * x_ref[rows, :] + mix_r, ln1_g_ref[...], ln1_b_ref[...])


def _mixer_kernel(tiles_per_seq, *refs):
    (w_in_hbm, w_out_hbm, *io_refs, h_even_ref, h_odd_ref, x_even_ref, x_odd_ref, xb_ref, ubuf_ref, cbuf_ref,
     state_ref, w_in_ref, w_out_ref, stage_in_ref, stage_out_ref, sem_ref) = refs
    n = pl.program_id(0)
    seq_tile = lax.rem(jnp.maximum(n - 1, 0), tiles_per_seq)

    @pl.when(n == 0)
    def _():
        _stage_weight_to_bf16(w_in_hbm, w_in_ref, stage_in_ref, sem_ref)
        _stage_weight_to_bf16(w_out_hbm, w_out_ref, stage_out_ref, sem_ref)
        h_odd_ref[...] = jnp.zeros(h_odd_ref.shape, F32)
        x_odd_ref[...] = jnp.zeros(x_odd_ref.shape, F32)
        ubuf_ref[...] = jnp.zeros(ubuf_ref.shape, F32)

    @pl.when(seq_tile == 0)
    def _():
        ubuf_ref[:, 0:2 * CONV_HALO, :] = jnp.zeros((ubuf_ref.shape[0], 2 * CONV_HALO, LANES), F32)
        state_ref[...] = jnp.zeros(state_ref.shape, F32)

    def step(h_next_ref, h_ref, x_keep_ref, x_ref):
        _mixer_step(h_next_ref, h_ref, x_keep_ref, x_ref, seq_tile, w_in_ref, w_out_ref, *io_refs, xb_ref,
                    ubuf_ref, cbuf_ref, state_ref)

    @pl.when(lax.rem(n, 2) == 0)
    def _():
        step(h_even_ref, h_odd_ref, x_even_ref, x_odd_ref)

    @pl.when(lax.rem(n, 2) == 1)
    def _():
        step(h_odd_ref, h_even_ref, x_odd_ref, x_even_ref)


def _mixer(x, w_in, conv_w, conv_b, cln_g, cln_b, gn_g, gn_b, w_out, ln1_g, ln1_b, tabs):
    B, S, D = x.shape
    T = MIXER_TILE
    tiles_per_seq = S // T
    n_tiles = B * tiles_per_seq

    def tile_index(tile):
        return lax.div(tile, tiles_per_seq), lax.rem(tile, tiles_per_seq)

    def next_tile(n):
        b, s = tile_index(jnp.minimum(n, n_tiles - 1))
        return b, s, 0

    def this_tile(n):
        b, s = tile_index(jnp.maximum(n - 1, 0))
        return b, s, 0

    const2 = lambda n: (0, 0)
    const3 = lambda n: (0, 0, 0)
    row = lambda width: pl.BlockSpec((1, width), const2)
    pos = lambda: pl.BlockSpec((S, RET_HEAD_DIM), const2, pipeline_mode=pl.Buffered(1))
    tab = lambda: pl.BlockSpec((RET_HEADS, RET_CHUNK, RET_HEAD_DIM), const3)
    conv_w_s = conv_w.reshape(CONV_KERNEL, N_STRIPS, LANES).transpose(1, 0, 2)
    conv_b_s = conv_b.reshape(N_STRIPS, 1, LANES)
    return pl.pallas_call(
        functools.partial(_mixer_kernel, tiles_per_seq),
        grid=(n_tiles + 1,),
        in_specs=[
            pl.BlockSpec(memory_space=pl.ANY), pl.BlockSpec(memory_space=pl.ANY),
            pl.BlockSpec((1, T, D), next_tile),
            pl.BlockSpec((N_STRIPS, CONV_KERNEL, LANES), const3),
            pl.BlockSpec((N_STRIPS, 1, LANES), const3),
            row(CONV_WIDTH), row(CONV_WIDTH), row(RET_WIDTH), row(RET_WIDTH),
            row(D), row(D),
            pos(), pos(),
            tab(), tab(), tab(), tab(),
        ],
        out_specs=pl.BlockSpec((1, T, D), this_tile),
        out_shape=jax.ShapeDtypeStruct((B, S, D), F32),
        scratch_shapes=[
            pltpu.VMEM((N_PIECES, T, PROJ_COLS), F32),
            pltpu.VMEM((N_PIECES, T, PROJ_COLS), F32),
            pltpu.VMEM((T, D), F32),
            pltpu.VMEM((T, D), F32),
            pltpu.VMEM((T, D), BF16),
            pltpu.VMEM((N_STRIPS // 2, 2 * (CONV_HALO + T), LANES), F32),
            pltpu.VMEM((N_STRIPS, T, LANES), F32),
            pltpu.VMEM((RET_HEADS, RET_HEAD_DIM, RET_HEAD_DIM), F32),
            pltpu.VMEM((D, IN_COLS), BF16),
            pltpu.VMEM((D, D), BF16),
            pltpu.VMEM(_stage_shape(IN_COLS), F32),
            pltpu.VMEM(_stage_shape(D), F32),
            pltpu.SemaphoreType.DMA((2,)),
        ],
        compiler_params=pltpu.CompilerParams(dimension_semantics=("arbitrary",),
                                             vmem_limit_bytes=VMEM_LIMIT_BYTES),
        name="mixer",
    )(w_in, w_out, x, conv_w_s, conv_b_s, cln_g, cln_b, gn_g, gn_b, ln1_g, ln1_b, *tabs)


def _attn_mlp_kernel(x_ref, k_ref, v_ref, wq_hbm, wo_hbm, ln2_g_ref, ln2_b_ref, wup_hbm, wdn_hbm,
                     ln3_g_ref, ln3_b_ref, o_ref, wq_ref, wo_ref, wup_ref, wdn_ref, stage_d_ref, stage_ff_ref,
                     sem_ref):
    @pl.when((pl.program_id(0) == 0) & (pl.program_id(1) == 0))
    def _():
        _stage_weight_to_bf16(wq_hbm, wq_ref, stage_d_ref, sem_ref)
        _stage_weight_to_bf16(wo_hbm, wo_ref, stage_d_ref, sem_ref)
        _stage_weight_to_bf16(wup_hbm, wup_ref, stage_ff_ref, sem_ref)
        _stage_weight_to_bf16(wdn_hbm, wdn_ref, stage_d_ref, sem_ref)

    x = x_ref[0]
    T = x.shape[0]
    halves = (slice(0, T // 2), slice(T // 2, T))
    q = (_dot(x.astype(BF16), wq_ref[...]) * (XATTN_HEAD_DIM ** -0.5)).astype(BF16)
    head_cols = [slice(h * XATTN_HEAD_DIM, (h + 1) * XATTN_HEAD_DIM) for h in range(XATTN_HEADS)]
    scores = [_dot_nt(q[:, hs], k_ref[0, :, hs]) for hs in head_cols]
    heads = []
    for s, hs in zip(scores, head_cols):
        p = jnp.exp(s - jnp.max(s, axis=-1, keepdims=True))
        l = jnp.sum(p, axis=-1, keepdims=True)
        o = _dot(p.astype(BF16), v_ref[0, :, hs])
        heads.append((o * (1.0 / l)).astype(BF16))
    attn = jnp.concatenate(heads, axis=1)
    xa = [_dot(attn[rows], wo_ref[...]) for rows in halves]
    x2 = [_layer_norm(DN_ALPHA * x[rows] + xa_r, ln2_g_ref[...], ln2_b_ref[...]) for rows, xa_r in zip(halves, xa)]
    x2b_halves = [x2_r.astype(BF16) for x2_r in x2]
    n_chunks = D_FF // FF_CHUNK
    up_cols = [slice(j * FF_CHUNK, (j + 1) * FF_CHUNK) for j in range(n_chunks)]

    def act(up):
        r = jnp.maximum(up, 0.0)
        return (r * r).astype(BF16)

    up = jnp.concatenate([_dot(x2b_r, wup_ref[:, up_cols[0]]) for x2b_r in x2b_halves], axis=0)
    x2b = jnp.concatenate(x2b_halves, axis=0)
    ff = None
    for j in range(1, n_chunks):
        up_next = _dot(x2b, wup_ref[:, up_cols[j]])
        part = _dot(act(up), wdn_ref[up_cols[j - 1], :])
        ff = part if ff is None else ff + part
        up = up_next
    hid = act(up)
    x2 = jnp.concatenate(x2, axis=0)
    for r0 in range(0, T, T // TAIL_SPLIT):
        rows = slice(r0, r0 + T // TAIL_SPLIT)
        ff_r = ff[rows] + _dot(hid[rows], wdn_ref[up_cols[-1], :])
        o_ref[0, rows, :] = _layer_norm(DN_ALPHA * x2[rows] + ff_r, ln3_g_ref[...], ln3_b_ref[...])


def _attn_mlp(x1, k, v, w_xq, w_xo, ln2_g, ln2_b, w_up, w_down, ln3_g, ln3_b):
    B, S, D = x1.shape
    T = ATTN_TILE
    const2 = lambda b, s: (0, 0)
    row = lambda n: pl.BlockSpec((1, n), const2)
    hbm = pl.BlockSpec(memory_space=pl.ANY)
    return pl.pallas_call(
        _attn_mlp_kernel,
        grid=(B, S // T),
        in_specs=[
            pl.BlockSpec((1, T, D), lambda b, s: (b, s, 0)),
            pl.BlockSpec((1, N_MEM, D), lambda b, s: (b, 0, 0)),
            pl.BlockSpec((1, N_MEM, D), lambda b, s: (b, 0, 0)),
            hbm, hbm,
            row(D), row(D),
            hbm, hbm,
            row(D), row(D),
        ],
        out_specs=pl.BlockSpec((1, T, D), lambda b, s: (b, s, 0)),
        out_shape=jax.ShapeDtypeStruct((B, S, D), F32),
        scratch_shapes=[
            pltpu.VMEM((D, D), BF16), pltpu.VMEM((D, D), BF16),
            pltpu.VMEM((D, D_FF), BF16), pltpu.VMEM((D_FF, D), BF16),
            pltpu.VMEM(_stage_shape(D), F32),
            pltpu.VMEM(_stage_shape(D_FF), F32),
            pltpu.SemaphoreType.DMA((2,)),
        ],
        compiler_params=pltpu.CompilerParams(dimension_semantics=("arbitrary", "arbitrary"),
                                             vmem_limit_bytes=VMEM_LIMIT_BYTES),
        name="attn_mlp",
    )(x1, k, v, w_xq, w_xo, ln2_g, ln2_b, w_up, w_down, ln3_g, ln3_b)


def _position_tables(seq):
    half = RET_HEAD_DIM // 2
    pos = jnp.arange(seq, dtype=F32)
    inv = ROPE_BASE ** (-jnp.linspace(0.0, 1.0, half, dtype=F32))
    ang = pos[:, None] * inv[None, :]
    cos, sin = jnp.cos(ang), jnp.sin(ang)
    cos2 = jnp.concatenate([cos, cos], axis=-1)
    sin2 = jnp.concatenate([-sin, sin], axis=-1)
    qs = RET_HEAD_DIM ** -0.5
    C = RET_CHUNK
    log_g = jnp.log(1.0 - 2.0 ** (-5.0 - jnp.arange(RET_HEADS, dtype=F32)))
    idx = jnp.arange(C, dtype=F32)
    rel = idx[:, None] - idx[None, :]
    decay = jnp.where(rel >= 0, jnp.exp(log_g[:, None, None] * jnp.maximum(rel, 0.0)), 0.0)
    q_dec = jnp.exp(log_g[:, None] * (idx + 1.0))
    k_dec = jnp.exp(log_g[:, None] * (C - 1.0 - idx))
    chunk_dec = jnp.exp(log_g * C)
    bcast = lambda t: jnp.broadcast_to(t, (RET_HEADS, C, RET_HEAD_DIM))
    return (cos2, sin2, decay * qs,
            bcast(q_dec[:, :, None]) * qs, bcast(k_dec[:, :, None]), bcast(chunk_dec[:, None, None]))


def kernel(x, mem, w_in, conv_w, conv_b, conv_ln_g, conv_ln_b, ret_gn_g, ret_gn_b, w_out,
           ln1_g, ln1_b, w_xq, w_xk, w_xv, w_xo, ln2_g, ln2_b, w_up, w_down, ln3_g, ln3_b):
    assert w_in.shape[0] == DEPTH == 1
    assert RET_CHUNK == RET_HEAD_DIM
    tabs = _position_tables(x.shape[1])
    k, v = _kv_proj(mem, w_xk[0], w_xv[0])
    x1 = _mixer(x, w_in[0], conv_w[0], conv_b, conv_ln_g, conv_ln_b, ret_gn_g, ret_gn_b, w_out[0],
                ln1_g, ln1_b, tabs)
    return _attn_mlp(x1, k, v, w_xq[0], w_xo[0], ln2_g, ln2_b, w_up[0], w_down[0], ln3_g, ln3_b)
```

```python
import functools

import jax
import jax.numpy as jnp
from jax import lax
from jax.experimental import pallas as pl
from jax.experimental.pallas import tpu as pltpu

D_MODEL = 1024
N_MEM = 256
CONV_WIDTH = 512
CONV_KERNEL = 31
RET_WIDTH = 512
RET_HEADS = 4
RET_HEAD_DIM = 128
RET_CHUNK = 128
ROPE_BASE = 10000.0
IN_COLS = 2 * CONV_WIDTH + 4 * RET_WIDTH
XATTN_HEADS = 4
XATTN_HEAD_DIM = 256
D_FF = 4 * D_MODEL
LN_EPS = 1e-5
DEPTH = 1
DN_ALPHA = (2.0 * DEPTH) ** 0.25

CONV_HALO = 32
SUBLANES, LANES = 8, 128
CONV_ROWS = 128
N_STRIPS = CONV_WIDTH // LANES
PROJ_COLS = 512
N_PIECES = IN_COLS // PROJ_COLS
P_A, P_B, P_Q, P_K, P_V, P_G = range(N_PIECES)
MIXER_TILE = 512
ATTN_TILE = 1024
FF_CHUNK = 1024
TAIL_SPLIT = 4
VMEM_LIMIT_BYTES = 58 * 1024 * 1024
STAGE_BYTES = 1024 * 1024

BF16 = jnp.bfloat16
F32 = jnp.float32


def _dot(a, b):
    return jnp.dot(a, b, preferred_element_type=F32)


def _dot_nt(a, b):
    return lax.dot_general(a, b, (((1,), (1,)), ((), ())), preferred_element_type=F32)


def _sigmoid(x):
    return 1.0 / (1.0 + jnp.exp(-x))


def _layer_norm(x, g, b):
    mu = jnp.mean(x, axis=-1, keepdims=True)
    xc = x - mu
    var = jnp.mean(xc * xc, axis=-1, keepdims=True)
    return xc * lax.rsqrt(var + LN_EPS) * g + b


def _stage_shape(cols):
    rows = 1 << ((STAGE_BYTES // (4 * cols)).bit_length() - 1)
    return (2, rows, cols)


def _stage_weight_to_bf16(w_hbm_ref, w_bf16_ref, stage_ref, sem_ref):
    rows = stage_ref.shape[1]
    n_chunks = w_hbm_ref.shape[0] // rows
    assert n_chunks * rows == w_hbm_ref.shape[0] and stage_ref.shape[2] == w_hbm_ref.shape[1]

    def chunk_copy(i, slot):
        return pltpu.make_async_copy(w_hbm_ref.at[pl.ds(i * rows, rows)], stage_ref.at[slot], sem_ref.at[slot])

    chunk_copy(0, 0).start()

    def body(i, carry):
        slot = lax.rem(i, 2)

        @pl.when(i + 1 < n_chunks)
        def _():
            chunk_copy(i + 1, 1 - slot).start()

        chunk_copy(i, slot).wait()
        w_bf16_ref[pl.ds(pl.multiple_of(i * rows, rows), rows), :] = stage_ref[slot].astype(BF16)
        return carry

    lax.fori_loop(0, n_chunks, body, 0)


def _kv_kernel(mem_ref, wk_ref, wv_ref, k_ref, v_ref, wk_bf_ref, wv_bf_ref):
    @pl.when(pl.program_id(0) == 0)
    def _():
        wk_bf_ref[...] = wk_ref[...].astype(BF16)
        wv_bf_ref[...] = wv_ref[...].astype(BF16)

    m = mem_ref[0].astype(BF16)
    k_ref[0] = _dot(m, wk_bf_ref[...]).astype(BF16)
    v_ref[0] = _dot(m, wv_bf_ref[...]).astype(BF16)


def _kv_proj(mem, w_xk, w_xv):
    B = mem.shape[0]
    const = lambda b: (0, 0)
    return pl.pallas_call(
        _kv_kernel,
        grid=(B,),
        in_specs=[
            pl.BlockSpec((1, N_MEM, D_MODEL), lambda b: (b, 0, 0)),
            pl.BlockSpec((D_MODEL, D_MODEL), const),
            pl.BlockSpec((D_MODEL, D_MODEL), const),
        ],
        out_specs=[
            pl.BlockSpec((1, N_MEM, D_MODEL), lambda b: (b, 0, 0)),
            pl.BlockSpec((1, N_MEM, D_MODEL), lambda b: (b, 0, 0)),
        ],
        out_shape=[jax.ShapeDtypeStruct((B, N_MEM, D_MODEL), BF16)] * 2,
        scratch_shapes=[pltpu.VMEM((D_MODEL, D_MODEL), BF16)] * 2,
        compiler_params=pltpu.CompilerParams(dimension_semantics=("arbitrary",),
                                             vmem_limit_bytes=VMEM_LIMIT_BYTES),
        name="kv_proj",
    )(mem, w_xk, w_xv)


def _mixer_step(h_next_ref, h_ref, x_keep_ref, x_ref, seq_tile, w_in_ref, w_out_ref, x_next_ref, conv_w_ref,
                conv_b_ref, cln_g_ref, cln_b_ref, gn_g_ref, gn_b_ref, ln1_g_ref, ln1_b_ref, cos_ref, sin_ref,
                decay_ref, qdec_ref, kdec_ref, cdec_ref, o_ref, xb_ref, ubuf_ref, cbuf_ref, state_ref):
    T = MIXER_TILE

    def project(p):
        h_next_ref[p] = _dot(xb_ref[...], w_in_ref[:, p * PROJ_COLS:(p + 1) * PROJ_COLS])

    def strip_rows(slab_parity, t_start, n):
        slab, parity = slab_parity
        return slab, pl.ds(2 * t_start + parity, n, stride=2), slice(None)

    base = CONV_HALO - (CONV_KERNEL - 1)

    def conv_strip(c):
        where = divmod(c, 2)
        n_acc = CONV_ROWS // SUBLANES
        for t0 in range(0, T, CONV_ROWS):
            acc = [jnp.broadcast_to(conv_b_ref[c], (SUBLANES, LANES))] * n_acc
            for r in range(SUBLANES):
                taps = range(r, CONV_KERNEL, SUBLANES)
                wts = [jnp.broadcast_to(conv_w_ref[c, j:j + 1, :], (SUBLANES, LANES)) for j in taps]
                for m in range(n_acc + len(taps) - 1):
                    win = ubuf_ref[strip_rows(where, t0 + base + r + SUBLANES * m, SUBLANES)]
                    for a, wt in enumerate(wts):
                        if 0 <= m - a < n_acc:
                            acc[m - a] = acc[m - a] + win * wt
            cbuf_ref[c, t0:t0 + CONV_ROWS, :] = jnp.concatenate(acc, axis=0)

    for c in range(N_STRIPS):
        pl.when(pl.program_id(0) >= 0)(functools.partial(conv_strip, c))
    ubuf_ref[:, 0:2 * CONV_HALO, :] = ubuf_ref[:, 2 * T:2 * (T + CONV_HALO), :]
    x_next = x_next_ref[0]
    xb_ref[...] = x_next.astype(BF16)
    project(P_A)
    x_keep_ref[...] = x_next
    project(P_B)
    row_groups = [slice(r0, r0 + T // TAIL_SPLIT) for r0 in range(0, T, T // TAIL_SPLIT)]
    mix_conv = []
    for rows in row_groups:
        conv = jnp.concatenate([cbuf_ref[c, rows, :] for c in range(N_STRIPS)], axis=1)
        cn = _layer_norm(conv, cln_g_ref[...], cln_b_ref[...])
        conv_out = (cn * _sigmoid(cn)).astype(BF16)
        mix_conv.append(_dot(conv_out, w_out_ref[0:CONV_WIDTH, :]))
    project(P_Q)
    u_next = h_next_ref[P_A] * _sigmoid(h_next_ref[P_B])
    for c in range(N_STRIPS):
        ubuf_ref[strip_rows(divmod(c, 2), CONV_HALO, T)] = u_next[:, c * LANES:(c + 1) * LANES]

    pos_rows = pl.ds(pl.multiple_of(seq_tile * T, T), T)
    cos, sin = cos_ref[pos_rows, :], sin_ref[pos_rows, :]
    half = RET_HEAD_DIM // 2
    C = RET_CHUNK
    n_chunks = T // C
    head_cols = [slice(h * RET_HEAD_DIM, (h + 1) * RET_HEAD_DIM) for h in range(RET_HEADS)]
    chunk_rows = [slice(c * C, (c + 1) * C) for c in range(n_chunks)]
    q, k, v, scores, kv = [], [], [], [], []
    for h, hs in enumerate(head_cols):
        qh = h_ref[P_Q, :, hs]
        kh = h_ref[P_K, :, hs]
        qh = qh * cos + pltpu.roll(qh, half, 1) * sin
        kh = kh * cos + pltpu.roll(kh, half, 1) * sin
        q.append(qh)
        k.append(kh)
        v.append(h_ref[P_V, :, hs].astype(BF16))
        scores.append([(_dot_nt(qh[rows].astype(BF16), kh[rows].astype(BF16)) * decay_ref[h]).astype(BF16)
                       for rows in chunk_rows])
    for h in range(RET_HEADS):
        kv.append([_dot((k[h][rows] * kdec_ref[h]).T.astype(BF16), v[h][rows]) for rows in chunk_rows])
    project(P_K)
    states = [state_ref[h] for h in range(RET_HEADS)]
    ys = [[] for _ in range(RET_HEADS)]
    for c, rows in enumerate(chunk_rows):
        for h in range(RET_HEADS):
            lhs = jnp.concatenate([scores[h][c], (q[h][rows] * qdec_ref[h]).astype(BF16)], axis=1)
            rhs = jnp.concatenate([v[h][rows], states[h].astype(BF16)], axis=0)
            ys[h].append(_dot(lhs, rhs))
            states[h] = states[h] * cdec_ref[h] + kv[h][c]
    ret_heads = []
    for h, hs in enumerate(head_cols):
        state_ref[h] = states[h]
        y = jnp.concatenate(ys[h], axis=0)
        mu = jnp.mean(y, axis=-1, keepdims=True)
        yc = y - mu
        var = jnp.mean(yc * yc, axis=-1, keepdims=True)
        yn = yc * lax.rsqrt(var + LN_EPS)
        yn = yn * gn_g_ref[:, hs] + gn_b_ref[:, hs]
        gh = h_ref[P_G, :, hs]
        ret_heads.append((gh * _sigmoid(gh) * yn).astype(BF16))

    project(P_V)
    ret_out = jnp.concatenate(ret_heads, axis=1)
    mix = [mix_conv_r + _dot(ret_out[rows], w_out_ref[CONV_WIDTH:, :])
           for rows, mix_conv_r in zip(row_groups, mix_conv)]
    project(P_G)
    for rows, mix_r in zip(row_groups, mix):
        o_ref[0, rows, :] = _layer_norm(DN_ALPHA * x_ref[rows, :] + mix_r, ln1_g_ref[...], ln1_b_ref[...])


def _mixer_kernel(tiles_per_seq, *refs):
    (w_in_hbm, w_out_hbm, *io_refs, h_even_ref, h_odd_ref, x_even_ref, x_odd_ref, xb_ref, ubuf_ref, cbuf_ref,
     state_ref, w_in_ref, w_out_ref, stage_in_ref, stage_out_ref, sem_ref) = refs
    n = pl.program_id(0)
    seq_tile = lax.rem(jnp.maximum(n - 1, 0), tiles_per_seq)

    @pl.when(n == 0)
    def _():
        _stage_weight_to_bf16(w_in_hbm, w_in_ref, stage_in_ref, sem_ref)
        _stage_weight_to_bf16(w_out_hbm, w_out_ref, stage_out_ref, sem_ref)
        h_odd_ref[...] = jnp.zeros(h_odd_ref.shape, F32)
        x_odd_ref[...] = jnp.zeros(x_odd_ref.shape, F32)
        ubuf_ref[...] = jnp.zeros(ubuf_ref.shape, F32)

    @pl.when(seq_tile == 0)
    def _():
        ubuf_ref[:, 0:2 * CONV_HALO, :] = jnp.zeros((ubuf_ref.shape[0], 2 * CONV_HALO, LANES), F32)
        state_ref[...] = jnp.zeros(state_ref.shape, F32)

    def step(h_next_ref, h_ref, x_keep_ref, x_ref):
        _mixer_step(h_next_ref, h_ref, x_keep_ref, x_ref, seq_tile, w_in_ref, w_out_ref, *io_refs, xb_ref,
                    ubuf_ref, cbuf_ref, state_ref)

    @pl.when(lax.rem(n, 2) == 0)
    def _():
        step(h_even_ref, h_odd_ref, x_even_ref, x_odd_ref)

    @pl.when(lax.rem(n, 2) == 1)
    def _():
        step(h_odd_ref, h_even_ref, x_odd_ref, x_even_ref)


def _mixer(x, w_in, conv_w, conv_b, cln_g, cln_b, gn_g, gn_b, w_out, ln1_g, ln1_b, tabs):
    B, S, D = x.shape
    T = MIXER_TILE
    tiles_per_seq = S // T
    n_tiles = B * tiles_per_seq

    def tile_index(tile):
        return lax.div(tile, tiles_per_seq), lax.rem(tile, tiles_per_seq)

    def next_tile(n):
        b, s = tile_index(jnp.minimum(n, n_tiles - 1))
        return b, s, 0

    def this_tile(n):
        b, s = tile_index(jnp.maximum(n - 1, 0))
        return b, s, 0

    const2 = lambda n: (0, 0)
    const3 = lambda n: (0, 0, 0)
    row = lambda width: pl.BlockSpec((1, width), const2)
    pos = lambda: pl.BlockSpec((S, RET_HEAD_DIM), const2, pipeline_mode=pl.Buffered(1))
    tab = lambda: pl.BlockSpec((RET_HEADS, RET_CHUNK, RET_HEAD_DIM), const3)
    conv_w_s = conv_w.reshape(CONV_KERNEL, N_STRIPS, LANES).transpose(1, 0, 2)
    conv_b_s = conv_b.reshape(N_STRIPS, 1, LANES)
    return pl.pallas_call(
        functools.partial(_mixer_kernel, tiles_per_seq),
        grid=(n_tiles + 1,),
        in_specs=[
            pl.BlockSpec(memory_space=pl.ANY), pl.BlockSpec(memory_space=pl.ANY),
            pl.BlockSpec((1, T, D), next_tile),
            pl.BlockSpec((N_STRIPS, CONV_KERNEL, LANES), const3),
            pl.BlockSpec((N_STRIPS, 1, LANES), const3),
            row(CONV_WIDTH), row(CONV_WIDTH), row(RET_WIDTH), row(RET_WIDTH),
            row(D), row(D),
            pos(), pos(),
            tab(), tab(), tab(), tab(),
        ],
        out_specs=pl.BlockSpec((1, T, D), this_tile),
        out_shape=jax.ShapeDtypeStruct((B, S, D), F32),
        scratch_shapes=[
            pltpu.VMEM((N_PIECES, T, PROJ_COLS), F32),
            pltpu.VMEM((N_PIECES, T, PROJ_COLS), F32),
            pltpu.VMEM((T, D), F32),
            pltpu.VMEM((T, D), F32),
            pltpu.VMEM((T, D), BF16),
            pltpu.VMEM((N_STRIPS // 2, 2 * (CONV_HALO + T), LANES), F32),
            pltpu.VMEM((N_STRIPS, T, LANES), F32),
            pltpu.VMEM((RET_HEADS, RET_HEAD_DIM, RET_HEAD_DIM), F32),
            pltpu.VMEM((D, IN_COLS), BF16),
            pltpu.VMEM((D, D), BF16),
            pltpu.VMEM(_stage_shape(IN_COLS), F32),
            pltpu.VMEM(_stage_shape(D), F32),
            pltpu.SemaphoreType.DMA((2,)),
        ],
        compiler_params=pltpu.CompilerParams(dimension_semantics=("arbitrary",),
                                             vmem_limit_bytes=VMEM_LIMIT_BYTES),
        name="mixer",
    )(w_in, w_out, x, conv_w_s, conv_b_s, cln_g, cln_b, gn_g, gn_b, ln1_g, ln1_b, *tabs)


def _attn_mlp_kernel(x_ref, k_ref, v_ref, wq_hbm, wo_hbm, ln2_g_ref, ln2_b_ref, wup_hbm, wdn_hbm,
                     ln3_g_ref, ln3_b_ref, o_ref, wq_ref, wo_ref, wup_ref, wdn_ref, stage_d_ref, stage_ff_ref,
                     sem_ref):
    @pl.when((pl.program_id(0) == 0) & (pl.program_id(1) == 0))
    def _():
        _stage_weight_to_bf16(wq_hbm, wq_ref, stage_d_ref, sem_ref)
        _stage_weight_to_bf16(wo_hbm, wo_ref, stage_d_ref, sem_ref)
        _stage_weight_to_bf16(wup_hbm, wup_ref, stage_ff_ref, sem_ref)
        _stage_weight_to_bf16(wdn_hbm, wdn_ref, stage_d_ref, sem_ref)

    x = x_ref[0]
    T = x.shape[0]
    halves = (slice(0, T // 2), slice(T // 2, T))
    q = (_dot(x.astype(BF16), wq_ref[...]) * (XATTN_HEAD_DIM ** -0.5)).astype(BF16)
    head_cols = [slice(h * XATTN_HEAD_DIM, (h + 1) * XATTN_HEAD_DIM) for h in range(XATTN_HEADS)]
    scores = [_dot_nt(q[:, hs], k_ref[0, :, hs]) for hs in head_cols]
    heads = []
    for s, hs in zip(scores, head_cols):
        p = jnp.exp(s - jnp.max(s, axis=-1, keepdims=True))
        l = jnp.sum(p, axis=-1, keepdims=True)
        o = _dot(p.astype(BF16), v_ref[0, :, hs])
        heads.append((o * (1.0 / l)).astype(BF16))
    attn = jnp.concatenate(heads, axis=1)
    xa = [_dot(attn[rows], wo_ref[...]) for rows in halves]
    x2 = [_layer_norm(DN_ALPHA * x[rows] + xa_r, ln2_g_ref[...], ln2_b_ref[...]) for rows, xa_r in zip(halves, xa)]
    x2b_halves = [x2_r.astype(BF16) for x2_r in x2]
    n_chunks = D_FF // FF_CHUNK
    up_cols = [slice(j * FF_CHUNK, (j + 1) * FF_CHUNK) for j in range(n_chunks)]

    def act(up):
        r = jnp.maximum(up, 0.0)
        return (r * r).astype(BF16)

    up = jnp.concatenate([_dot(x2b_r, wup_ref[:, up_cols[0]]) for x2b_r in x2b_halves], axis=0)
    x2b = jnp.concatenate(x2b_halves, axis=0)
    ff = None
    for j in range(1, n_chunks):
        up_next = _dot(x2b, wup_ref[:, up_cols[j]])
        part = _dot(act(up), wdn_ref[up_cols[j - 1], :])
        ff = part if ff is None else ff + part
        up = up_next
    hid = act(up)
    x2 = jnp.concatenate(x2, axis=0)
    for r0 in range(0, T, T // TAIL_SPLIT):
        rows = slice(r0, r0 + T // TAIL_SPLIT)
        ff_r = ff[rows] + _dot(hid[rows], wdn_ref[up_cols[-1], :])
        o_ref[0, rows, :] = _layer_norm(DN_ALPHA * x2[rows] + ff_r, ln3_g_ref[...], ln3_b_ref[...])


def _attn_mlp(x1, k, v, w_xq, w_xo, ln2_g, ln2_b, w_up, w_down, ln3_g, ln3_b):
    B, S, D = x1.shape
    T = ATTN_TILE
    const2 = lambda b, s: (0, 0)
    row = lambda n: pl.BlockSpec((1, n), const2)
    hbm = pl.BlockSpec(memory_space=pl.ANY)
    return pl.pallas_call(
        _attn_mlp_kernel,
        grid=(B, S // T),
        in_specs=[
            pl.BlockSpec((1, T, D), lambda b, s: (b, s, 0)),
            pl.BlockSpec((1, N_MEM, D), lambda b, s: (b, 0, 0)),
            pl.BlockSpec((1, N_MEM, D), lambda b, s: (b, 0, 0)),
            hbm, hbm,
            row(D), row(D),
            hbm, hbm,
            row(D), row(D),
        ],
        out_specs=pl.BlockSpec((1, T, D), lambda b, s: (b, s, 0)),
        out_shape=jax.ShapeDtypeStruct((B, S, D), F32),
        scratch_shapes=[
            pltpu.VMEM((D, D), BF16), pltpu.VMEM((D, D), BF16),
            pltpu.VMEM((D, D_FF), BF16), pltpu.VMEM((D_FF, D), BF16),
            pltpu.VMEM(_stage_shape(D), F32),
            pltpu.VMEM(_stage_shape(D_FF), F32),
            pltpu.SemaphoreType.DMA((2,)),
        ],
        compiler_params=pltpu.CompilerParams(dimension_semantics=("arbitrary", "arbitrary"),
                                             vmem_limit_bytes=VMEM_LIMIT_BYTES),
        name="attn_mlp",
    )(x1, k, v, w_xq, w_xo, ln2_g, ln2_b, w_up, w_down, ln3_g, ln3_b)


def _position_tables(seq):
    half = RET_HEAD_DIM // 2
    pos = jnp.arange(seq, dtype=F32)
    inv = ROPE_BASE ** (-jnp.linspace(0.0, 1.0, half, dtype=F32))
    ang = pos[:, None] * inv[None, :]
    cos, sin = jnp.cos(ang), jnp.sin(ang)
    cos2 = jnp.concatenate([cos, cos], axis=-1)
    sin2 = jnp.concatenate([-sin, sin], axis=-1)
    qs = RET_HEAD_DIM ** -0.5
    C = RET_CHUNK
    log_g = jnp.log(1.0 - 2.0 ** (-5.0 - jnp.arange(RET_HEADS, dtype=F32)))
    idx = jnp.arange(C, dtype=F32)
    rel = idx[:, None] - idx[None, :]
    decay = jnp.where(rel >= 0, jnp.exp(log_g[:, None, None] * jnp.maximum(rel, 0.0)), 0.0)
    q_dec = jnp.exp(log_g[:, None] * (idx + 1.0))
    k_dec = jnp.exp(log_g[:, None] * (C - 1.0 - idx))
    chunk_dec = jnp.exp(log_g * C)
    bcast = lambda t: jnp.broadcast_to(t, (RET_HEADS, C, RET_HEAD_DIM))
    return (cos2, sin2, decay * qs,
            bcast(q_dec[:, :, None]) * qs, bcast(k_dec[:, :, None]), bcast(chunk_dec[:, None, None]))


def kernel(x, mem, w_in, conv_w, conv_b, conv_ln_g, conv_ln_b, ret_gn_g, ret_gn_b, w_out,
           ln1_g, ln1_b, w_xq, w_xk, w_xv, w_xo, ln2_g, ln2_b, w_up, w_down, ln3_g, ln3_b):
    assert w_in.shape[0] == DEPTH == 1
    assert RET_CHUNK == RET_HEAD_DIM
    tabs = _position_tables(x.shape[1])
    k, v = _kv_proj(mem, w_xk[0], w_xv[0])
    x1 = _mixer(x, w_in[0], conv_w[0], conv_b, conv_ln_g, conv_ln_b, ret_gn_g, ret_gn_b, w_out[0],
                ln1_g, ln1_b, tabs)
    return _attn_mlp(x1, k, v, w_xq[0], w_xo[0], ln2_g, ln2_b, w_up[0], w_down[0], ln3_g, ln3_b)
```

```python
import functools

import jax
import jax.numpy as jnp
from jax import lax
from jax.experimental import pallas as pl
from jax.experimental.pallas import tpu as pltpu

D_MODEL = 1024
N_MEM = 256
CONV_WIDTH = 512
CONV_KERNEL = 31
RET_WIDTH = 512
RET_HEADS = 4
RET_HEAD_DIM = 128
RET_CHUNK = 128
ROPE_BASE = 10000.0
IN_COLS = 2 * CONV_WIDTH + 4 * RET_WIDTH
XATTN_HEADS = 4
XATTN_HEAD_DIM = 256
D_FF = 4 * D_MODEL
LN_EPS = 1e-5
DEPTH = 1
DN_ALPHA = (2.0 * DEPTH) ** 0.25

CONV_HALO = 32
SUBLANES, LANES = 8, 128
CONV_ROWS = 128
N_STRIPS = CONV_WIDTH // LANES
PROJ_COLS = 512
N_PIECES = IN_COLS // PROJ_COLS
P_A, P_B, P_Q, P_K, P_V, P_G = range(N_PIECES)
MIXER_TILE = 512
ATTN_TILE = 512
FF_CHUNK = 1024
TAIL_SPLIT = 4
VMEM_LIMIT_BYTES = 56 * 1024 * 1024
STAGE_BYTES = 1024 * 1024

BF16 = jnp.bfloat16
F32 = jnp.float32


def _dot(a, b):
    return jnp.dot(a, b, preferred_element_type=F32)


def _dot_nt(a, b):
    return lax.dot_general(a, b, (((1,), (1,)), ((), ())), preferred_element_type=F32)


def _sigmoid(x):
    return 1.0 / (1.0 + jnp.exp(-x))


def _layer_norm(x, g, b):
    mu = jnp.mean(x, axis=-1, keepdims=True)
    xc = x - mu
    var = jnp.mean(xc * xc, axis=-1, keepdims=True)
    return xc * lax.rsqrt(var + LN_EPS) * g + b


def _stage_shape(cols):
    rows = 1 << ((STAGE_BYTES // (4 * cols)).bit_length() - 1)
    return (2, rows, cols)


def _stage_weight_to_bf16(w_hbm_ref, w_bf16_ref, stage_ref, sem_ref):
    rows = stage_ref.shape[1]
    n_chunks = w_hbm_ref.shape[0] // rows
    assert n_chunks * rows == w_hbm_ref.shape[0] and stage_ref.shape[2] == w_hbm_ref.shape[1]

    def chunk_copy(i, slot):
        return pltpu.make_async_copy(w_hbm_ref.at[pl.ds(i * rows, rows)], stage_ref.at[slot], sem_ref.at[slot])

    chunk_copy(0, 0).start()

    def body(i, carry):
        slot = lax.rem(i, 2)

        @pl.when(i + 1 < n_chunks)
        def _():
            chunk_copy(i + 1, 1 - slot).start()

        chunk_copy(i, slot).wait()
        w_bf16_ref[pl.ds(pl.multiple_of(i * rows, rows), rows), :] = stage_ref[slot].astype(BF16)
        return carry

    lax.fori_loop(0, n_chunks, body, 0)


def _kv_kernel(mem_ref, wk_ref, wv_ref, k_ref, v_ref, wk_bf_ref, wv_bf_ref):
    @pl.when(pl.program_id(0) == 0)
    def _():
        wk_bf_ref[...] = wk_ref[...].astype(BF16)
        wv_bf_ref[...] = wv_ref[...].astype(BF16)

    m = mem_ref[0].astype(BF16)
    k_ref[0] = _dot(m, wk_bf_ref[...]).astype(BF16)
    v_ref[0] = _dot(m, wv_bf_ref[...]).astype(BF16)


def _kv_proj(mem, w_xk, w_xv):
    B = mem.shape[0]
    const = lambda b: (0, 0)
    return pl.pallas_call(
        _kv_kernel,
        grid=(B,),
        in_specs=[
            pl.BlockSpec((1, N_MEM, D_MODEL), lambda b: (b, 0, 0)),
            pl.BlockSpec((D_MODEL, D_MODEL), const),
            pl.BlockSpec((D_MODEL, D_MODEL), const),
        ],
        out_specs=[
            pl.BlockSpec((1, N_MEM, D_MODEL), lambda b: (b, 0, 0)),
            pl.BlockSpec((1, N_MEM, D_MODEL), lambda b: (b, 0, 0)),
        ],
        out_shape=[jax.ShapeDtypeStruct((B, N_MEM, D_MODEL), BF16)] * 2,
        scratch_shapes=[pltpu.VMEM((D_MODEL, D_MODEL), BF16)] * 2,
        compiler_params=pltpu.CompilerParams(dimension_semantics=("arbitrary",),
                                             vmem_limit_bytes=VMEM_LIMIT_BYTES),
        name="kv_proj",
    )(mem, w_xk, w_xv)


def _mixer_step(h_next_ref, h_ref, x_keep_ref, x_ref, seq_tile, w_in_ref, w_out_ref, x_next_ref, conv_w_ref,
                conv_b_ref, cln_g_ref, cln_b_ref, gn_g_ref, gn_b_ref, ln1_g_ref, ln1_b_ref, cos_ref, sin_ref,
                decay_ref, qdec_ref, kdec_ref, cdec_ref, o_ref, xb_ref, ubuf_ref, cbuf_ref, state_ref):
    T = MIXER_TILE
    x_next = x_next_ref[0]
    xb_ref[...] = x_next.astype(BF16)

    def project(p):
        h_next_ref[p] = _dot(xb_ref[...], w_in_ref[:, p * PROJ_COLS:(p + 1) * PROJ_COLS])

    def strip_rows(slab_parity, t_start, n):
        slab, parity = slab_parity
        return slab, pl.ds(2 * t_start + parity, n, stride=2), slice(None)

    base = CONV_HALO - (CONV_KERNEL - 1)

    def conv_strip(c):
        where = divmod(c, 2)
        n_acc = CONV_ROWS // SUBLANES
        for t0 in range(0, T, CONV_ROWS):
            acc = [jnp.broadcast_to(conv_b_ref[c], (SUBLANES, LANES))] * n_acc
            for r in range(SUBLANES):
                taps = range(r, CONV_KERNEL, SUBLANES)
                wts = [jnp.broadcast_to(conv_w_ref[c, j:j + 1, :], (SUBLANES, LANES)) for j in taps]
                for m in range(n_acc + len(taps) - 1):
                    win = ubuf_ref[strip_rows(where, t0 + base + r + SUBLANES * m, SUBLANES)]
                    for a, wt in enumerate(wts):
                        if 0 <= m - a < n_acc:
                            acc[m - a] = acc[m - a] + win * wt
            cbuf_ref[c, t0:t0 + CONV_ROWS, :] = jnp.concatenate(acc, axis=0)
        if c < CONV_PIECES:
            project(c)

    for c in range(N_STRIPS):
        pl.when(pl.program_id(0) >= 0)(functools.partial(conv_strip, c))
    ubuf_ref[:, 0:2 * CONV_HALO, :] = ubuf_ref[:, 2 * T:2 * (T + CONV_HALO), :]
    x_keep_ref[...] = x_next
    row_groups = [slice(r0, r0 + T // TAIL_SPLIT) for r0 in range(0, T, T // TAIL_SPLIT)]
    mix_conv = []
    for rows in row_groups:
        conv = jnp.concatenate([cbuf_ref[c, rows, :] for c in range(N_STRIPS)], axis=1)
        cn = _layer_norm(conv, cln_g_ref[...], cln_b_ref[...])
        conv_out = (cn * _sigmoid(cn)).astype(BF16)
        mix_conv.append(_dot(conv_out, w_out_ref[0:CONV_WIDTH, :]))
    project(P_Q)
    u_next = h_next_ref[P_A] * _sigmoid(h_next_ref[P_B])
    for c in range(N_STRIPS):
        ubuf_ref[strip_rows(divmod(c, 2), CONV_HALO, T)] = u_next[:, c * LANES:(c + 1) * LANES]

    pos_rows = pl.ds(pl.multiple_of(seq_tile * T, T), T)
    cos, sin = cos_ref[pos_rows, :], sin_ref[pos_rows, :]
    half = RET_HEAD_DIM // 2
    C = RET_CHUNK
    n_chunks = T // C
    head_cols = [slice(h * RET_HEAD_DIM, (h + 1) * RET_HEAD_DIM) for h in range(RET_HEADS)]
    chunk_rows = [slice(c * C, (c + 1) * C) for c in range(n_chunks)]
    q, k, v, scores, kv = [], [], [], [], []
    for h, hs in enumerate(head_cols):
        qh = h_ref[P_Q, :, hs]
        kh = h_ref[P_K, :, hs]
        qh = qh * cos + pltpu.roll(qh, half, 1) * sin
        kh = kh * cos + pltpu.roll(kh, half, 1) * sin
        q.append(qh)
        k.append(kh)
        v.append(h_ref[P_V, :, hs].astype(BF16))
        scores.append([(_dot_nt(qh[rows].astype(BF16), kh[rows].astype(BF16)) * decay_ref[h]).astype(BF16)
                       for rows in chunk_rows])
    for h in range(RET_HEADS):
        kv.append([_dot((k[h][rows] * kdec_ref[h]).T.astype(BF16), v[h][rows]) for rows in chunk_rows])
    project(P_K)
    states = [state_ref[h] for h in range(RET_HEADS)]
    ys = [[] for _ in range(RET_HEADS)]
    for c, rows in enumerate(chunk_rows):
        for h in range(RET_HEADS):
            lhs = jnp.concatenate([scores[h][c], (q[h][rows] * qdec_ref[h]).astype(BF16)], axis=1)
            rhs = jnp.concatenate([v[h][rows], states[h].astype(BF16)], axis=0)
            ys[h].append(_dot(lhs, rhs))
            states[h] = states[h] * cdec_ref[h] + kv[h][c]
    ret_heads = []
    for h, hs in enumerate(head_cols):
        state_ref[h] = states[h]
        y = jnp.concatenate(ys[h], axis=0)
        mu = jnp.mean(y, axis=-1, keepdims=True)
        yc = y - mu
        var = jnp.mean(yc * yc, axis=-1, keepdims=True)
        yn = yc * lax.rsqrt(var + LN_EPS)
        yn = yn * gn_g_ref[:, hs] + gn_b_ref[:, hs]
        gh = h_ref[P_G, :, hs]
        ret_heads.append((gh * _sigmoid(gh) * yn).astype(BF16))

    project(P_V)
    ret_out = jnp.concatenate(ret_heads, axis=1)
    mix = [mix_conv_r + _dot(ret_out[rows], w_out_ref[CONV_WIDTH:, :])
           for rows, mix_conv_r in zip(row_groups, mix_conv)]
    project(P_G)
    for rows, mix_r in zip(row_groups, mix):
        o_ref[0, rows, :] = _layer_norm(DN_ALPHA * x_ref[rows, :] + mix_r, ln1_g_ref[...], ln1_b_ref[...])


CONV_PIECES = 2


def _mixer_kernel(tiles_per_seq, *refs):
    (w_in_hbm, w_out_hbm, *io_refs, h_even_ref, h_odd_ref, x_even_ref, x_odd_ref, xb_ref, ubuf_ref, cbuf_ref,
     state_ref, w_in_ref, w_out_ref, stage_in_ref, stage_out_ref, sem_ref) = refs
    n = pl.program_id(0)
    seq_tile = lax.rem(jnp.maximum(n - 1, 0), tiles_per_seq)

    @pl.when(n == 0)
    def _():
        _stage_weight_to_bf16(w_in_hbm, w_in_ref, stage_in_ref, sem_ref)
        _stage_weight_to_bf16(w_out_hbm, w_out_ref, stage_out_ref, sem_ref)
        h_odd_ref[...] = jnp.zeros(h_odd_ref.shape, F32)
        x_odd_ref[...] = jnp.zeros(x_odd_ref.shape, F32)
        ubuf_ref[...] = jnp.zeros(ubuf_ref.shape, F32)

    @pl.when(seq_tile == 0)
    def _():
        ubuf_ref[:, 0:2 * CONV_HALO, :] = jnp.zeros((ubuf_ref.shape[0], 2 * CONV_HALO, LANES), F32)
        state_ref[...] = jnp.zeros(state_ref.shape, F32)

    def step(h_next_ref, h_ref, x_keep_ref, x_ref):
        _mixer_step(h_next_ref, h_ref, x_keep_ref, x_ref, seq_tile, w_in_ref, w_out_ref, *io_refs, xb_ref,
                    ubuf_ref, cbuf_ref, state_ref)

    @pl.when(lax.rem(n, 2) == 0)
    def _():
        step(h_even_ref, h_odd_ref, x_even_ref, x_odd_ref)

    @pl.when(lax.rem(n, 2) == 1)
    def _():
        step(h_odd_ref, h_even_ref, x_odd_ref, x_even_ref)


def _mixer(x, w_in, conv_w, conv_b, cln_g, cln_b, gn_g, gn_b, w_out, ln1_g, ln1_b, tabs):
    B, S, D = x.shape
    T = MIXER_TILE
    tiles_per_seq = S // T
    n_tiles = B * tiles_per_seq

    def tile_index(tile):
        return lax.div(tile, tiles_per_seq), lax.rem(tile, tiles_per_seq)

    def next_tile(n):
        b, s = tile_index(jnp.minimum(n, n_tiles - 1))
        return b, s, 0

    def this_tile(n):
        b, s = tile_index(jnp.maximum(n - 1, 0))
        return b, s, 0

    const2 = lambda n: (0, 0)
    const3 = lambda n: (0, 0, 0)
    row = lambda width: pl.BlockSpec((1, width), const2)
    pos = lambda: pl.BlockSpec((S, RET_HEAD_DIM), const2, pipeline_mode=pl.Buffered(1))
    tab = lambda: pl.BlockSpec((RET_HEADS, RET_CHUNK, RET_HEAD_DIM), const3)
    conv_w_s = conv_w.reshape(CONV_KERNEL, N_STRIPS, LANES).transpose(1, 0, 2)
    conv_b_s = conv_b.reshape(N_STRIPS, 1, LANES)
    return pl.pallas_call(
        functools.partial(_mixer_kernel, tiles_per_seq),
        grid=(n_tiles + 1,),
        in_specs=[
            pl.BlockSpec(memory_space=pl.ANY), pl.BlockSpec(memory_space=pl.ANY),
            pl.BlockSpec((1, T, D), next_tile),
            pl.BlockSpec((N_STRIPS, CONV_KERNEL, LANES), const3),
            pl.BlockSpec((N_STRIPS, 1, LANES), const3),
            row(CONV_WIDTH), row(CONV_WIDTH), row(RET_WIDTH), row(RET_WIDTH),
            row(D), row(D),
            pos(), pos(),
            tab(), tab(), tab(), tab(),
        ],
        out_specs=pl.BlockSpec((1, T, D), this_tile),
        out_shape=jax.ShapeDtypeStruct((B, S, D), F32),
        scratch_shapes=[
            pltpu.VMEM((N_PIECES, T, PROJ_COLS), F32),
            pltpu.VMEM((N_PIECES, T, PROJ_COLS), F32),
            pltpu.VMEM((T, D), F32),
            pltpu.VMEM((T, D), F32),
            pltpu.VMEM((T, D), BF16),
            pltpu.VMEM((N_STRIPS // 2, 2 * (CONV_HALO + T), LANES), F32),
            pltpu.VMEM((N_STRIPS, T, LANES), F32),
            pltpu.VMEM((RET_HEADS, RET_HEAD_DIM, RET_HEAD_DIM), F32),
            pltpu.VMEM((D, IN_COLS), BF16),
            pltpu.VMEM((D, D), BF16),
            pltpu.VMEM(_stage_shape(IN_COLS), F32),
            pltpu.VMEM(_stage_shape(D), F32),
            pltpu.SemaphoreType.DMA((2,)),
        ],
        compiler_params=pltpu.CompilerParams(dimension_semantics=("arbitrary",),
                                             vmem_limit_bytes=VMEM_LIMIT_BYTES),
        name="mixer",
    )(w_in, w_out, x, conv_w_s, conv_b_s, cln_g, cln_b, gn_g, gn_b, ln1_g, ln1_b, *tabs)


def _attn_mlp_kernel(x_ref, k_ref, v_ref, wq_hbm, wo_hbm, ln2_g_ref, ln2_b_ref, wup_hbm, wdn_hbm,
                     ln3_g_ref, ln3_b_ref, o_ref, wq_ref, wo_ref, wup_ref, wdn_ref, stage_d_ref, stage_ff_ref,
                     sem_ref):
    @pl.when((pl.program_id(0) == 0) & (pl.program_id(1) == 0))
    def _():
        _stage_weight_to_bf16(wq_hbm, wq_ref, stage_d_ref, sem_ref)
        _stage_weight_to_bf16(wo_hbm, wo_ref, stage_d_ref, sem_ref)
        _stage_weight_to_bf16(wup_hbm, wup_ref, stage_ff_ref, sem_ref)
        _stage_weight_to_bf16(wdn_hbm, wdn_ref, stage_d_ref, sem_ref)

    x = x_ref[0]
    T = x.shape[0]
    halves = (slice(0, T // 2), slice(T // 2, T))
    q = (_dot(x.astype(BF16), wq_ref[...]) * (XATTN_HEAD_DIM ** -0.5)).astype(BF16)
    head_cols = [slice(h * XATTN_HEAD_DIM, (h + 1) * XATTN_HEAD_DIM) for h in range(XATTN_HEADS)]
    scores = [_dot_nt(q[:, hs], k_ref[0, :, hs]) for hs in head_cols]
    heads = []
    for s, hs in zip(scores, head_cols):
        p = jnp.exp(s - jnp.max(s, axis=-1, keepdims=True))
        l = jnp.sum(p, axis=-1, keepdims=True)
        o = _dot(p.astype(BF16), v_ref[0, :, hs])
        heads.append((o * (1.0 / l)).astype(BF16))
    attn = jnp.concatenate(heads, axis=1)
    xa = [_dot(attn[rows], wo_ref[...]) for rows in halves]
    x2 = [_layer_norm(DN_ALPHA * x[rows] + xa_r, ln2_g_ref[...], ln2_b_ref[...]) for rows, xa_r in zip(halves, xa)]
    x2b_halves = [x2_r.astype(BF16) for x2_r in x2]
    n_chunks = D_FF // FF_CHUNK
    up_cols = [slice(j * FF_CHUNK, (j + 1) * FF_CHUNK) for j in range(n_chunks)]

    def act(up):
        r = jnp.maximum(up, 0.0)
        return (r * r).astype(BF16)

    up = jnp.concatenate([_dot(x2b_r, wup_ref[:, up_cols[0]]) for x2b_r in x2b_halves], axis=0)
    x2b = jnp.concatenate(x2b_halves, axis=0)
    ff = None
    for j in range(1, n_chunks):
        up_next = _dot(x2b, wup_ref[:, up_cols[j]])
        part = _dot(act(up), wdn_ref[up_cols[j - 1], :])
        ff = part if ff is None else ff + part
        up = up_next
    hid = act(up)
    x2 = jnp.concatenate(x2, axis=0)
    for r0 in range(0, T, T // TAIL_SPLIT):
        rows = slice(r0, r0 + T // TAIL_SPLIT)
        ff_r = ff[rows] + _dot(hid[rows], wdn_ref[up_cols[-1], :])
        o_ref[0, rows, :] = _layer_norm(DN_ALPHA * x2[rows] + ff_r, ln3_g_ref[...], ln3_b_ref[...])


def _attn_mlp(x1, k, v, w_xq, w_xo, ln2_g, ln2_b, w_up, w_down, ln3_g, ln3_b):
    B, S, D = x1.shape
    T = ATTN_TILE
    const2 = lambda b, s: (0, 0)
    row = lambda n: pl.BlockSpec((1, n), const2)
    hbm = pl.BlockSpec(memory_space=pl.ANY)
    return pl.pallas_call(
        _attn_mlp_kernel,
        grid=(B, S // T),
        in_specs=[
            pl.BlockSpec((1, T, D), lambda b, s: (b, s, 0)),
            pl.BlockSpec((1, N_MEM, D), lambda b, s: (b, 0, 0)),
            pl.BlockSpec((1, N_MEM, D), lambda b, s: (b, 0, 0)),
            hbm, hbm,
            row(D), row(D),
            hbm, hbm,
            row(D), row(D),
        ],
        out_specs=pl.BlockSpec((1, T, D), lambda b, s: (b, s, 0)),
        out_shape=jax.ShapeDtypeStruct((B, S, D), F32),
        scratch_shapes=[
            pltpu.VMEM((D, D), BF16), pltpu.VMEM((D, D), BF16),
            pltpu.VMEM((D, D_FF), BF16), pltpu.VMEM((D_FF, D), BF16),
            pltpu.VMEM(_stage_shape(D), F32),
            pltpu.VMEM(_stage_shape(D_FF), F32),
            pltpu.SemaphoreType.DMA((2,)),
        ],
        compiler_params=pltpu.CompilerParams(dimension_semantics=("arbitrary", "arbitrary"),
                                             vmem_limit_bytes=VMEM_LIMIT_BYTES),
        name="attn_mlp",
    )(x1, k, v, w_xq, w_xo, ln2_g, ln2_b, w_up, w_down, ln3_g, ln3_b)


def _position_tables(seq):
    half = RET_HEAD_DIM // 2
    pos = jnp.arange(seq, dtype=F32)
    inv = ROPE_BASE ** (-jnp.linspace(0.0, 1.0, half, dtype=F32))
    ang = pos[:, None] * inv[None, :]
    cos, sin = jnp.cos(ang), jnp.sin(ang)
    cos2 = jnp.concatenate([cos, cos], axis=-1)
    sin2 = jnp.concatenate([-sin, sin], axis=-1)
    qs = RET_HEAD_DIM ** -0.5
    C = RET_CHUNK
    log_g = jnp.log(1.0 - 2.0 ** (-5.0 - jnp.arange(RET_HEADS, dtype=F32)))
    idx = jnp.arange(C, dtype=F32)
    rel = idx[:, None] - idx[None, :]
    decay = jnp.where(rel >= 0, jnp.exp(log_g[:, None, None] * jnp.maximum(rel, 0.0)), 0.0)
    q_dec = jnp.exp(log_g[:, None] * (idx + 1.0))
    k_dec = jnp.exp(log_g[:, None] * (C - 1.0 - idx))
    chunk_dec = jnp.exp(log_g * C)
    bcast = lambda t: jnp.broadcast_to(t, (RET_HEADS, C, RET_HEAD_DIM))
    return (cos2, sin2, decay * qs,
            bcast(q_dec[:, :, None]) * qs, bcast(k_dec[:, :, None]), bcast(chunk_dec[:, None, None]))


def kernel(x, mem, w_in, conv_w, conv_b, conv_ln_g, conv_ln_b, ret_gn_g, ret_gn_b, w_out,
           ln1_g, ln1_b, w_xq, w_xk, w_xv, w_xo, ln2_g, ln2_b, w_up, w_down, ln3_g, ln3_b):
    assert w_in.shape[0] == DEPTH == 1
    assert RET_CHUNK == RET_HEAD_DIM
    tabs = _position_tables(x.shape[1])
    k, v = _kv_proj(mem, w_xk[0], w_xv[0])
    x1 = _mixer(x, w_in[0], conv_w[0], conv_b, conv_ln_g, conv_ln_b, ret_gn_g, ret_gn_b, w_out[0],
                ln1_g, ln1_b, tabs)
    return _attn_mlp(x1, k, v, w_xq[0], w_xo[0], ln2_g, ln2_b, w_up[0], w_down[0], ln3_g, ln3_b)
```

```python
import functools

import jax
import jax.numpy as jnp
from jax import lax
from jax.experimental import pallas as pl
from jax.experimental.pallas import tpu as pltpu

D_MODEL = 1024
N_MEM = 256
CONV_WIDTH = 512
CONV_KERNEL = 31
RET_WIDTH = 512
RET_HEADS = 4
RET_HEAD_DIM = 128
RET_CHUNK = 128
ROPE_BASE = 10000.0
IN_COLS = 2 * CONV_WIDTH + 4 * RET_WIDTH
XATTN_HEADS = 4
XATTN_HEAD_DIM = 256
D_FF = 4 * D_MODEL
LN_EPS = 1e-5
DEPTH = 1
DN_ALPHA = (2.0 * DEPTH) ** 0.25

CONV_HALO = 32
SUBLANES, LANES = 8, 128
CONV_ROWS = 128
N_STRIPS = CONV_WIDTH // LANES
PROJ_COLS = 512
N_PIECES = IN_COLS // PROJ_COLS
P_A, P_B, P_Q, P_K, P_V, P_G = range(N_PIECES)
MIXER_TILE = 512
ATTN_TILE = 512
FF_CHUNK = 1024
TAIL_SPLIT = 4
VMEM_LIMIT_BYTES = 56 * 1024 * 1024
STAGE_BYTES = 1024 * 1024

BF16 = jnp.bfloat16
F32 = jnp.float32


def _dot(a, b):
    return jnp.dot(a, b, preferred_element_type=F32)


def _dot_nt(a, b):
    return lax.dot_general(a, b, (((1,), (1,)), ((), ())), preferred_element_type=F32)


def _sigmoid(x):
    return 1.0 / (1.0 + jnp.exp(-x))


def _layer_norm(x, g, b):
    mu = jnp.mean(x, axis=-1, keepdims=True)
    xc = x - mu
    var = jnp.mean(xc * xc, axis=-1, keepdims=True)
    return xc * lax.rsqrt(var + LN_EPS) * g + b


def _stage_shape(cols):
    rows = 1 << ((STAGE_BYTES // (4 * cols)).bit_length() - 1)
    return (2, rows, cols)


def _stage_weight_to_bf16(w_hbm_ref, w_bf16_ref, stage_ref, sem_ref):
    rows = stage_ref.shape[1]
    n_chunks = w_hbm_ref.shape[0] // rows
    assert n_chunks * rows == w_hbm_ref.shape[0] and stage_ref.shape[2] == w_hbm_ref.shape[1]

    def chunk_copy(i, slot):
        return pltpu.make_async_copy(w_hbm_ref.at[pl.ds(i * rows, rows)], stage_ref.at[slot], sem_ref.at[slot])

    chunk_copy(0, 0).start()

    def body(i, carry):
        slot = lax.rem(i, 2)

        @pl.when(i + 1 < n_chunks)
        def _():
            chunk_copy(i + 1, 1 - slot).start()

        chunk_copy(i, slot).wait()
        w_bf16_ref[pl.ds(pl.multiple_of(i * rows, rows), rows), :] = stage_ref[slot].astype(BF16)
        return carry

    lax.fori_loop(0, n_chunks, body, 0)


def _kv_kernel(mem_ref, wk_ref, wv_ref, k_ref, v_ref, wk_bf_ref, wv_bf_ref):
    @pl.when(pl.program_id(0) == 0)
    def _():
        wk_bf_ref[...] = wk_ref[...].astype(BF16)
        wv_bf_ref[...] = wv_ref[...].astype(BF16)

    m = mem_ref[0].astype(BF16)
    k_ref[0] = _dot(m, wk_bf_ref[...]).astype(BF16)
    v_ref[0] = _dot(m, wv_bf_ref[...]).astype(BF16)


def _kv_proj(mem, w_xk, w_xv):
    B = mem.shape[0]
    const = lambda b: (0, 0)
    return pl.pallas_call(
        _kv_kernel,
        grid=(B,),
        in_specs=[
            pl.BlockSpec((1, N_MEM, D_MODEL), lambda b: (b, 0, 0)),
            pl.BlockSpec((D_MODEL, D_MODEL), const),
            pl.BlockSpec((D_MODEL, D_MODEL), const),
        ],
        out_specs=[
            pl.BlockSpec((1, N_MEM, D_MODEL), lambda b: (b, 0, 0)),
            pl.BlockSpec((1, N_MEM, D_MODEL), lambda b: (b, 0, 0)),
        ],
        out_shape=[jax.ShapeDtypeStruct((B, N_MEM, D_MODEL), BF16)] * 2,
        scratch_shapes=[pltpu.VMEM((D_MODEL, D_MODEL), BF16)] * 2,
        compiler_params=pltpu.CompilerParams(dimension_semantics=("arbitrary",),
                                             vmem_limit_bytes=VMEM_LIMIT_BYTES),
        name="kv_proj",
    )(mem, w_xk, w_xv)


def _mixer_step(h_next_ref, h_ref, x_keep_ref, x_ref, seq_tile, w_in_ref, w_out_ref, x_next_ref, conv_w_ref,
                conv_b_ref, cln_g_ref, cln_b_ref, gn_g_ref, gn_b_ref, ln1_g_ref, ln1_b_ref, cos_ref, sin_ref,
                decay_ref, qdec_ref, kdec_ref, cdec_ref, o_ref, xb_ref, ubuf_ref, cbuf_ref, state_ref,
                do_project=True, do_finish=True):
    T = MIXER_TILE

    def project(p):
        if do_project:
            h_next_ref[p] = _dot(xb_ref[...], w_in_ref[:, p * PROJ_COLS:(p + 1) * PROJ_COLS])

    def strip_rows(slab_parity, t_start, n):
        slab, parity = slab_parity
        return slab, pl.ds(2 * t_start + parity, n, stride=2), slice(None)

    base = CONV_HALO - (CONV_KERNEL - 1)

    def conv_strip(c):
        where = divmod(c, 2)
        n_acc = CONV_ROWS // SUBLANES
        for t0 in range(0, T, CONV_ROWS):
            acc = [jnp.broadcast_to(conv_b_ref[c], (SUBLANES, LANES))] * n_acc
            for r in range(SUBLANES):
                taps = range(r, CONV_KERNEL, SUBLANES)
                wts = [jnp.broadcast_to(conv_w_ref[c, j:j + 1, :], (SUBLANES, LANES)) for j in taps]
                for m in range(n_acc + len(taps) - 1):
                    win = ubuf_ref[strip_rows(where, t0 + base + r + SUBLANES * m, SUBLANES)]
                    for a, wt in enumerate(wts):
                        if 0 <= m - a < n_acc:
                            acc[m - a] = acc[m - a] + win * wt
            cbuf_ref[c, t0:t0 + CONV_ROWS, :] = jnp.concatenate(acc, axis=0)

    if do_finish:
        for c in range(N_STRIPS):
            pl.when(pl.program_id(0) >= 0)(functools.partial(conv_strip, c))
        ubuf_ref[:, 0:2 * CONV_HALO, :] = ubuf_ref[:, 2 * T:2 * (T + CONV_HALO), :]
    if do_project:
        x_next = x_next_ref[0]
        xb_ref[...] = x_next.astype(BF16)
        project(P_A)
        x_keep_ref[...] = x_next
        project(P_B)
        u_next = h_next_ref[P_A] * _sigmoid(h_next_ref[P_B])
        for c in range(N_STRIPS):
            ubuf_ref[strip_rows(divmod(c, 2), CONV_HALO, T)] = u_next[:, c * LANES:(c + 1) * LANES]
    if not do_finish:
        for p in (P_Q, P_K, P_V, P_G):
            project(p)
        return
    conv = jnp.concatenate([cbuf_ref[c] for c in range(N_STRIPS)], axis=1)
    cn = _layer_norm(conv, cln_g_ref[...], cln_b_ref[...])
    conv_out = (cn * _sigmoid(cn)).astype(BF16)
    mix_conv = _dot(conv_out, w_out_ref[0:CONV_WIDTH, :])

    pos_rows = pl.ds(pl.multiple_of(seq_tile * T, T), T)
    cos, sin = cos_ref[pos_rows, :], sin_ref[pos_rows, :]
    half = RET_HEAD_DIM // 2
    C = RET_CHUNK
    n_chunks = T // C
    head_cols = [slice(h * RET_HEAD_DIM, (h + 1) * RET_HEAD_DIM) for h in range(RET_HEADS)]
    q, v, scores, kv = [], [], [], []
    project(P_Q)
    for h, hs in enumerate(head_cols):
        qh = h_ref[P_Q, :, hs]
        kh = h_ref[P_K, :, hs]
        qh = qh * cos + pltpu.roll(qh, half, 1) * sin
        kh = kh * cos + pltpu.roll(kh, half, 1) * sin
        vh = h_ref[P_V, :, hs].astype(BF16)
        q.append(qh)
        v.append(vh)
        decay, kdec = decay_ref[h], kdec_ref[h]
        scores_h, kv_h = [], []
        for c in range(n_chunks):
            rows = slice(c * C, (c + 1) * C)
            scores_h.append((_dot_nt(qh[rows].astype(BF16), kh[rows].astype(BF16)) * decay).astype(BF16))
            kv_h.append(_dot((kh[rows] * kdec).T.astype(BF16), vh[rows]))
        scores.append(scores_h)
        kv.append(kv_h)
    project(P_K)
    ret_heads = []
    for h, hs in enumerate(head_cols):
        qdec, cdec = qdec_ref[h], cdec_ref[h]
        state = state_ref[h]
        ys = []
        for c in range(n_chunks):
            rows = slice(c * C, (c + 1) * C)
            lhs = jnp.concatenate([scores[h][c], (q[h][rows] * qdec).astype(BF16)], axis=1)
            rhs = jnp.concatenate([v[h][rows], state.astype(BF16)], axis=0)
            ys.append(_dot(lhs, rhs))
            state = state * cdec + kv[h][c]
        state_ref[h] = state
        y = jnp.concatenate(ys, axis=0)
        mu = jnp.mean(y, axis=-1, keepdims=True)
        yc = y - mu
        var = jnp.mean(yc * yc, axis=-1, keepdims=True)
        yn = yc * lax.rsqrt(var + LN_EPS)
        yn = yn * gn_g_ref[:, hs] + gn_b_ref[:, hs]
        gh = h_ref[P_G, :, hs]
        ret_heads.append((gh * _sigmoid(gh) * yn).astype(BF16))

    project(P_V)
    ret_out = jnp.concatenate(ret_heads, axis=1)
    mix = [mix_conv[r0:r0 + T // TAIL_SPLIT] + _dot(ret_out[r0:r0 + T // TAIL_SPLIT], w_out_ref[CONV_WIDTH:, :])
           for r0 in range(0, T, T // TAIL_SPLIT)]
    project(P_G)
    for i, mix_r in enumerate(mix):
        rows = slice(i * (T // TAIL_SPLIT), (i + 1) * (T // TAIL_SPLIT))
        o_ref[0, rows, :] = _layer_norm(DN_ALPHA * x_ref[rows, :] + mix_r, ln1_g_ref[...], ln1_b_ref[...])


def _mixer_kernel(tiles_per_seq, n_tiles, *refs):
    (w_in_hbm, w_out_hbm, *io_refs, h_even_ref, h_odd_ref, x_even_ref, x_odd_ref, xb_ref, ubuf_ref, cbuf_ref,
     state_ref, w_in_ref, w_out_ref, stage_in_ref, stage_out_ref, sem_ref) = refs
    n = pl.program_id(0)
    seq_tile = lax.rem(jnp.maximum(n - 1, 0), tiles_per_seq)
    buffers = ((h_even_ref, x_even_ref), (h_odd_ref, x_odd_ref))

    def step(parity, **phases):
        (h_next_ref, x_keep_ref), (h_ref, x_ref) = buffers[parity], buffers[1 - parity]
        _mixer_step(h_next_ref, h_ref, x_keep_ref, x_ref, seq_tile, w_in_ref, w_out_ref, *io_refs, xb_ref,
                    ubuf_ref, cbuf_ref, state_ref, **phases)

    @pl.when(n == 0)
    def _():
        _stage_weight_to_bf16(w_in_hbm, w_in_ref, stage_in_ref, sem_ref)
        _stage_weight_to_bf16(w_out_hbm, w_out_ref, stage_out_ref, sem_ref)
        step(0, do_finish=False)

    @pl.when(seq_tile == 0)
    def _():
        ubuf_ref[:, 0:2 * CONV_HALO, :] = jnp.zeros((ubuf_ref.shape[0], 2 * CONV_HALO, LANES), F32)
        state_ref[...] = jnp.zeros(state_ref.shape, F32)

    inner = (n > 0) & (n < n_tiles)
    for parity in (0, 1):
        pl.when(inner & (lax.rem(n, 2) == parity))(functools.partial(step, parity))

    @pl.when(n == n_tiles)
    def _():
        step(n_tiles % 2, do_project=False)


def _mixer(x, w_in, conv_w, conv_b, cln_g, cln_b, gn_g, gn_b, w_out, ln1_g, ln1_b, tabs):
    B, S, D = x.shape
    T = MIXER_TILE
    tiles_per_seq = S // T
    n_tiles = B * tiles_per_seq

    def tile_index(tile):
        return lax.div(tile, tiles_per_seq), lax.rem(tile, tiles_per_seq)

    def next_tile(n):
        b, s = tile_index(jnp.minimum(n, n_tiles - 1))
        return b, s, 0

    def this_tile(n):
        b, s = tile_index(jnp.maximum(n - 1, 0))
        return b, s, 0

    const2 = lambda n: (0, 0)
    const3 = lambda n: (0, 0, 0)
    row = lambda width: pl.BlockSpec((1, width), const2)
    pos = lambda: pl.BlockSpec((S, RET_HEAD_DIM), const2, pipeline_mode=pl.Buffered(1))
    tab = lambda: pl.BlockSpec((RET_HEADS, RET_CHUNK, RET_HEAD_DIM), const3)
    conv_w_s = conv_w.reshape(CONV_KERNEL, N_STRIPS, LANES).transpose(1, 0, 2)
    conv_b_s = conv_b.reshape(N_STRIPS, 1, LANES)
    return pl.pallas_call(
        functools.partial(_mixer_kernel, tiles_per_seq, n_tiles),
        grid=(n_tiles + 1,),
        in_specs=[
            pl.BlockSpec(memory_space=pl.ANY), pl.BlockSpec(memory_space=pl.ANY),
            pl.BlockSpec((1, T, D), next_tile),
            pl.BlockSpec((N_STRIPS, CONV_KERNEL, LANES), const3),
            pl.BlockSpec((N_STRIPS, 1, LANES), const3),
            row(CONV_WIDTH), row(CONV_WIDTH), row(RET_WIDTH), row(RET_WIDTH),
            row(D), row(D),
            pos(), pos(),
            tab(), tab(), tab(), tab(),
        ],
        out_specs=pl.BlockSpec((1, T, D), this_tile),
        out_shape=jax.ShapeDtypeStruct((B, S, D), F32),
        scratch_shapes=[
            pltpu.VMEM((N_PIECES, T, PROJ_COLS), F32),
            pltpu.VMEM((N_PIECES, T, PROJ_COLS), F32),
            pltpu.VMEM((T, D), F32),
            pltpu.VMEM((T, D), F32),
            pltpu.VMEM((T, D), BF16),
            pltpu.VMEM((N_STRIPS // 2, 2 * (CONV_HALO + T), LANES), F32),
            pltpu.VMEM((N_STRIPS, T, LANES), F32),
            pltpu.VMEM((RET_HEADS, RET_HEAD_DIM, RET_HEAD_DIM), F32),
            pltpu.VMEM((D, IN_COLS), BF16),
            pltpu.VMEM((D, D), BF16),
            pltpu.VMEM(_stage_shape(IN_COLS), F32),
            pltpu.VMEM(_stage_shape(D), F32),
            pltpu.SemaphoreType.DMA((2,)),
        ],
        compiler_params=pltpu.CompilerParams(dimension_semantics=("arbitrary",),
                                             vmem_limit_bytes=VMEM_LIMIT_BYTES),
        name="mixer",
    )(w_in, w_out, x, conv_w_s, conv_b_s, cln_g, cln_b, gn_g, gn_b, ln1_g, ln1_b, *tabs)


def _attn_mlp_kernel(x_ref, k_ref, v_ref, wq_hbm, wo_hbm, ln2_g_ref, ln2_b_ref, wup_hbm, wdn_hbm,
                     ln3_g_ref, ln3_b_ref, o_ref, wq_ref, wo_ref, wup_ref, wdn_ref, stage_d_ref, stage_ff_ref,
                     sem_ref):
    @pl.when((pl.program_id(0) == 0) & (pl.program_id(1) == 0))
    def _():
        _stage_weight_to_bf16(wq_hbm, wq_ref, stage_d_ref, sem_ref)
        _stage_weight_to_bf16(wo_hbm, wo_ref, stage_d_ref, sem_ref)
        _stage_weight_to_bf16(wup_hbm, wup_ref, stage_ff_ref, sem_ref)
        _stage_weight_to_bf16(wdn_hbm, wdn_ref, stage_d_ref, sem_ref)

    x = x_ref[0]
    T = x.shape[0]
    halves = (slice(0, T // 2), slice(T // 2, T))
    q = (_dot(x.astype(BF16), wq_ref[...]) * (XATTN_HEAD_DIM ** -0.5)).astype(BF16)
    head_cols = [slice(h * XATTN_HEAD_DIM, (h + 1) * XATTN_HEAD_DIM) for h in range(XATTN_HEADS)]
    scores = [_dot_nt(q[:, hs], k_ref[0, :, hs]) for hs in head_cols]
    heads = []
    for s, hs in zip(scores, head_cols):
        p = jnp.exp(s - jnp.max(s, axis=-1, keepdims=True))
        l = jnp.sum(p, axis=-1, keepdims=True)
        o = _dot(p.astype(BF16), v_ref[0, :, hs])
        heads.append((o * (1.0 / l)).astype(BF16))
    attn = jnp.concatenate(heads, axis=1)
    xa = [_dot(attn[rows], wo_ref[...]) for rows in halves]
    x2 = [_layer_norm(DN_ALPHA * x[rows] + xa_r, ln2_g_ref[...], ln2_b_ref[...]) for rows, xa_r in zip(halves, xa)]
    x2b_halves = [x2_r.astype(BF16) for x2_r in x2]
    n_chunks = D_FF // FF_CHUNK
    up_cols = [slice(j * FF_CHUNK, (j + 1) * FF_CHUNK) for j in range(n_chunks)]

    def act(up):
        r = jnp.maximum(up, 0.0)
        return (r * r).astype(BF16)

    up = jnp.concatenate([_dot(x2b_r, wup_ref[:, up_cols[0]]) for x2b_r in x2b_halves], axis=0)
    x2b = jnp.concatenate(x2b_halves, axis=0)
    ff = None
    for j in range(1, n_chunks):
        up_next = _dot(x2b, wup_ref[:, up_cols[j]])
        part = _dot(act(up), wdn_ref[up_cols[j - 1], :])
        ff = part if ff is None else ff + part
        up = up_next
    hid = act(up)
    x2 = jnp.concatenate(x2, axis=0)
    for r0 in range(0, T, T // TAIL_SPLIT):
        rows = slice(r0, r0 + T // TAIL_SPLIT)
        ff_r = ff[rows] + _dot(hid[rows], wdn_ref[up_cols[-1], :])
        o_ref[0, rows, :] = _layer_norm(DN_ALPHA * x2[rows] + ff_r, ln3_g_ref[...], ln3_b_ref[...])


def _attn_mlp(x1, k, v, w_xq, w_xo, ln2_g, ln2_b, w_up, w_down, ln3_g, ln3_b):
    B, S, D = x1.shape
    T = ATTN_TILE
    const2 = lambda b, s: (0, 0)
    row = lambda n: pl.BlockSpec((1, n), const2)
    hbm = pl.BlockSpec(memory_space=pl.ANY)
    return pl.pallas_call(
        _attn_mlp_kernel,
        grid=(B, S // T),
        in_specs=[
            pl.BlockSpec((1, T, D), lambda b, s: (b, s, 0)),
            pl.BlockSpec((1, N_MEM, D), lambda b, s: (b, 0, 0)),
            pl.BlockSpec((1, N_MEM, D), lambda b, s: (b, 0, 0)),
            hbm, hbm,
            row(D), row(D),
            hbm, hbm,
            row(D), row(D),
        ],
        out_specs=pl.BlockSpec((1, T, D), lambda b, s: (b, s, 0)),
        out_shape=jax.ShapeDtypeStruct((B, S, D), F32),
        scratch_shapes=[
            pltpu.VMEM((D, D), BF16), pltpu.VMEM((D, D), BF16),
            pltpu.VMEM((D, D_FF), BF16), pltpu.VMEM((D_FF, D), BF16),
            pltpu.VMEM(_stage_shape(D), F32),
            pltpu.VMEM(_stage_shape(D_FF), F32),
            pltpu.SemaphoreType.DMA((2,)),
        ],
        compiler_params=pltpu.CompilerParams(dimension_semantics=("arbitrary", "arbitrary"),
                                             vmem_limit_bytes=VMEM_LIMIT_BYTES),
        name="attn_mlp",
    )(x1, k, v, w_xq, w_xo, ln2_g, ln2_b, w_up, w_down, ln3_g, ln3_b)


def _position_tables(seq):
    half = RET_HEAD_DIM // 2
    pos = jnp.arange(seq, dtype=F32)
    inv = ROPE_BASE ** (-jnp.linspace(0.0, 1.0, half, dtype=F32))
    ang = pos[:, None] * inv[None, :]
    cos, sin = jnp.cos(ang), jnp.sin(ang)
    cos2 = jnp.concatenate([cos, cos], axis=-1)
    sin2 = jnp.concatenate([-sin, sin], axis=-1)
    qs = RET_HEAD_DIM ** -0.5
    C = RET_CHUNK
    log_g = jnp.log(1.0 - 2.0 ** (-5.0 - jnp.arange(RET_HEADS, dtype=F32)))
    idx = jnp.arange(C, dtype=F32)
    rel = idx[:, None] - idx[None, :]
    decay = jnp.where(rel >= 0, jnp.exp(log_g[:, None, None] * jnp.maximum(rel, 0.0)), 0.0)
    q_dec = jnp.exp(log_g[:, None] * (idx + 1.0))
    k_dec = jnp.exp(log_g[:, None] * (C - 1.0 - idx))
    chunk_dec = jnp.exp(log_g * C)
    bcast = lambda t: jnp.broadcast_to(t, (RET_HEADS, C, RET_HEAD_DIM))
    return (cos2, sin2, decay * qs,
            bcast(q_dec[:, :, None]) * qs, bcast(k_dec[:, :, None]), bcast(chunk_dec[:, None, None]))


def kernel(x, mem, w_in, conv_w, conv_b, conv_ln_g, conv_ln_b, ret_gn_g, ret_gn_b, w_out,
           ln1_g, ln1_b, w_xq, w_xk, w_xv, w_xo, ln2_g, ln2_b, w_up, w_down, ln3_g, ln3_b):
    assert w_in.shape[0] == DEPTH == 1
    assert RET_CHUNK == RET_HEAD_DIM
    tabs = _position_tables(x.shape[1])
    k, v = _kv_proj(mem, w_xk[0], w_xv[0])
    x1 = _mixer(x, w_in[0], conv_w[0], conv_b, conv_ln_g, conv_ln_b, ret_gn_g, ret_gn_b, w_out[0],
                ln1_g, ln1_b, tabs)
    return _attn_mlp(x1, k, v, w_xq[0], w_xo[0], ln2_g, ln2_b, w_up[0], w_down[0], ln3_g, ln3_b)
```

```python
import functools

import jax
import jax.numpy as jnp
from jax import lax
from jax.experimental import pallas as pl
from jax.experimental.pallas import tpu as pltpu

D_MODEL = 1024
N_MEM = 256
CONV_WIDTH = 512
CONV_KERNEL = 31
RET_WIDTH = 512
RET_HEADS = 4
RET_HEAD_DIM = 128
RET_CHUNK = 128
ROPE_BASE = 10000.0
IN_COLS = 2 * CONV_WIDTH + 4 * RET_WIDTH
XATTN_HEADS = 4
XATTN_HEAD_DIM = 256
D_FF = 4 * D_MODEL
LN_EPS = 1e-5
DEPTH = 1
DN_ALPHA = (2.0 * DEPTH) ** 0.25

CONV_HALO = 32
SUBLANES, LANES = 8, 128
CONV_ROWS = 128
N_STRIPS = CONV_WIDTH // LANES
PROJ_COLS = 512
N_PIECES = IN_COLS // PROJ_COLS
P_A, P_B, P_Q, P_K, P_V, P_G = range(N_PIECES)
MIXER_TILE = 512
ATTN_TILE = 512
KV_BATCHES = 4
FF_CHUNK = 1024
TAIL_SPLIT = 4
VMEM_LIMIT_BYTES = 56 * 1024 * 1024
STAGE_BYTES = 1024 * 1024

BF16 = jnp.bfloat16
F32 = jnp.float32


def _dot(a, b):
    return jnp.dot(a, b, preferred_element_type=F32)


def _dot_nt(a, b):
    return lax.dot_general(a, b, (((1,), (1,)), ((), ())), preferred_element_type=F32)


def _sigmoid(x):
    return 1.0 / (1.0 + jnp.exp(-x))


def _layer_norm(x, g, b):
    mu = jnp.mean(x, axis=-1, keepdims=True)
    xc = x - mu
    var = jnp.mean(xc * xc, axis=-1, keepdims=True)
    return xc * lax.rsqrt(var + LN_EPS) * g + b


def _stage_shape(cols):
    rows = 1 << ((STAGE_BYTES // (4 * cols)).bit_length() - 1)
    return (2, rows, cols)


def _stage_weight_to_bf16(w_hbm_ref, w_bf16_ref, stage_ref, sem_ref):
    rows = stage_ref.shape[1]
    n_chunks = w_hbm_ref.shape[0] // rows
    assert n_chunks * rows == w_hbm_ref.shape[0] and stage_ref.shape[2] == w_hbm_ref.shape[1]

    def chunk_copy(i, slot):
        return pltpu.make_async_copy(w_hbm_ref.at[pl.ds(i * rows, rows)], stage_ref.at[slot], sem_ref.at[slot])

    chunk_copy(0, 0).start()

    def body(i, carry):
        slot = lax.rem(i, 2)

        @pl.when(i + 1 < n_chunks)
        def _():
            chunk_copy(i + 1, 1 - slot).start()

        chunk_copy(i, slot).wait()
        w_bf16_ref[pl.ds(pl.multiple_of(i * rows, rows), rows), :] = stage_ref[slot].astype(BF16)
        return carry

    lax.fori_loop(0, n_chunks, body, 0)


def _kv_kernel(mem_ref, wk_ref, wv_ref, k_ref, v_ref, wk_bf_ref, wv_bf_ref):
    @pl.when(pl.program_id(0) == 0)
    def _():
        wk_bf_ref[...] = wk_ref[...].astype(BF16)
        wv_bf_ref[...] = wv_ref[...].astype(BF16)

    m = mem_ref[...].reshape(KV_BATCHES * N_MEM, D_MODEL).astype(BF16)
    k_ref[...] = _dot(m, wk_bf_ref[...]).astype(BF16).reshape(k_ref.shape)
    v_ref[...] = _dot(m, wv_bf_ref[...]).astype(BF16).reshape(v_ref.shape)


def _kv_proj(mem, w_xk, w_xv):
    B = mem.shape[0]
    const = lambda b: (0, 0)
    batches = lambda: pl.BlockSpec((KV_BATCHES, N_MEM, D_MODEL), lambda b: (b, 0, 0))
    return pl.pallas_call(
        _kv_kernel,
        grid=(B // KV_BATCHES,),
        in_specs=[
            batches(),
            pl.BlockSpec((D_MODEL, D_MODEL), const),
            pl.BlockSpec((D_MODEL, D_MODEL), const),
        ],
        out_specs=[batches(), batches()],
        out_shape=[jax.ShapeDtypeStruct((B, N_MEM, D_MODEL), BF16)] * 2,
        scratch_shapes=[pltpu.VMEM((D_MODEL, D_MODEL), BF16)] * 2,
        compiler_params=pltpu.CompilerParams(dimension_semantics=("arbitrary",),
                                             vmem_limit_bytes=VMEM_LIMIT_BYTES),
        name="kv_proj",
    )(mem, w_xk, w_xv)


def _mixer_step(h_next_ref, h_ref, x_keep_ref, x_ref, seq_tile, w_in_ref, w_out_ref, x_next_ref, conv_w_ref,
                conv_b_ref, cln_g_ref, cln_b_ref, gn_g_ref, gn_b_ref, ln1_g_ref, ln1_b_ref, cos_ref, sin_ref,
                decay_ref, qdec_ref, kdec_ref, cdec_ref, o_ref, xb_ref, ubuf_ref, cbuf_ref, state_ref):
    T = MIXER_TILE

    def project(p):
        h_next_ref[p] = _dot(xb_ref[...], w_in_ref[:, p * PROJ_COLS:(p + 1) * PROJ_COLS])

    def strip_rows(slab_parity, t_start, n):
        slab, parity = slab_parity
        return slab, pl.ds(2 * t_start + parity, n, stride=2), slice(None)

    base = CONV_HALO - (CONV_KERNEL - 1)

    def conv_strip(c):
        where = divmod(c, 2)
        cols = slice(c * LANES, (c + 1) * LANES)
        n_acc = CONV_ROWS // SUBLANES
        for t0 in range(0, T, CONV_ROWS):
            acc = [jnp.broadcast_to(conv_b_ref[:, cols], (SUBLANES, LANES))] * n_acc
            for r in range(SUBLANES):
                taps = range(r, CONV_KERNEL, SUBLANES)
                wts = [jnp.broadcast_to(conv_w_ref[j:j + 1, cols], (SUBLANES, LANES)) for j in taps]
                for m in range(n_acc + len(taps) - 1):
                    win = ubuf_ref[strip_rows(where, t0 + base + r + SUBLANES * m, SUBLANES)]
                    for a, wt in enumerate(wts):
                        if 0 <= m - a < n_acc:
                            acc[m - a] = acc[m - a] + win * wt
            cbuf_ref[c, t0:t0 + CONV_ROWS, :] = jnp.concatenate(acc, axis=0)

    for c in range(N_STRIPS):
        pl.when(pl.program_id(0) >= 0)(functools.partial(conv_strip, c))
    ubuf_ref[:, 0:2 * CONV_HALO, :] = ubuf_ref[:, 2 * T:2 * (T + CONV_HALO), :]
    x_next = x_next_ref[0]
    xb_ref[...] = x_next.astype(BF16)
    project(P_A)
    x_keep_ref[...] = x_next
    project(P_B)
    u_next = h_next_ref[P_A] * _sigmoid(h_next_ref[P_B])
    for c in range(N_STRIPS):
        ubuf_ref[strip_rows(divmod(c, 2), CONV_HALO, T)] = u_next[:, c * LANES:(c + 1) * LANES]
    conv = jnp.concatenate([cbuf_ref[c] for c in range(N_STRIPS)], axis=1)
    cn = _layer_norm(conv, cln_g_ref[...], cln_b_ref[...])
    conv_out = (cn * _sigmoid(cn)).astype(BF16)
    mix_conv = _dot(conv_out, w_out_ref[0:CONV_WIDTH, :])

    pos_rows = pl.ds(pl.multiple_of(seq_tile * T, T), T)
    cos, sin = cos_ref[pos_rows, :], sin_ref[pos_rows, :]
    half = RET_HEAD_DIM // 2
    C = RET_CHUNK
    n_chunks = T // C
    head_cols = [slice(h * RET_HEAD_DIM, (h + 1) * RET_HEAD_DIM) for h in range(RET_HEADS)]
    q, v, scores, kv = [], [], [], []
    project(P_Q)
    for h, hs in enumerate(head_cols):
        qh = h_ref[P_Q, :, hs]
        kh = h_ref[P_K, :, hs]
        qh = qh * cos + pltpu.roll(qh, half, 1) * sin
        kh = kh * cos + pltpu.roll(kh, half, 1) * sin
        vh = h_ref[P_V, :, hs].astype(BF16)
        q.append(qh)
        v.append(vh)
        decay, kdec = decay_ref[h], kdec_ref[h]
        scores_h, kv_h = [], []
        for c in range(n_chunks):
            rows = slice(c * C, (c + 1) * C)
            scores_h.append((_dot_nt(qh[rows].astype(BF16), kh[rows].astype(BF16)) * decay).astype(BF16))
            kv_h.append(_dot((kh[rows] * kdec).T.astype(BF16), vh[rows]))
        scores.append(scores_h)
        kv.append(kv_h)
    project(P_K)
    ret_heads = []
    for h, hs in enumerate(head_cols):
        qdec, cdec = qdec_ref[h], cdec_ref[h]
        state = state_ref[h]
        ys = []
        for c in range(n_chunks):
            rows = slice(c * C, (c + 1) * C)
            lhs = jnp.concatenate([scores[h][c], (q[h][rows] * qdec).astype(BF16)], axis=1)
            rhs = jnp.concatenate([v[h][rows], state.astype(BF16)], axis=0)
            ys.append(_dot(lhs, rhs))
            state = state * cdec + kv[h][c]
        state_ref[h] = state
        y = jnp.concatenate(ys, axis=0)
        mu = jnp.mean(y, axis=-1, keepdims=True)
        yc = y - mu
        var = jnp.mean(yc * yc, axis=-1, keepdims=True)
        yn = yc * lax.rsqrt(var + LN_EPS)
        yn = yn * gn_g_ref[:, hs] + gn_b_ref[:, hs]
        gh = h_ref[P_G, :, hs]
        ret_heads.append((gh * _sigmoid(gh) * yn).astype(BF16))

    project(P_V)
    ret_out = jnp.concatenate(ret_heads, axis=1)
    mix = [mix_conv[r0:r0 + T // TAIL_SPLIT] + _dot(ret_out[r0:r0 + T // TAIL_SPLIT], w_out_ref[CONV_WIDTH:, :])
           for r0 in range(0, T, T // TAIL_SPLIT)]
    project(P_G)
    for i, mix_r in enumerate(mix):
        rows = slice(i * (T // TAIL_SPLIT), (i + 1) * (T // TAIL_SPLIT))
        o_ref[0, rows, :] = _layer_norm(DN_ALPHA * x_ref[rows, :] + mix_r, ln1_g_ref[...], ln1_b_ref[...])


def _mixer_kernel(tiles_per_seq, *refs):
    (w_in_hbm, w_out_hbm, *io_refs, h_even_ref, h_odd_ref, x_even_ref, x_odd_ref, xb_ref, ubuf_ref, cbuf_ref,
     state_ref, w_in_ref, w_out_ref, stage_in_ref, stage_out_ref, sem_ref) = refs
    n = pl.program_id(0)
    seq_tile = lax.rem(jnp.maximum(n - 1, 0), tiles_per_seq)

    @pl.when(n == 0)
    def _():
        _stage_weight_to_bf16(w_in_hbm, w_in_ref, stage_in_ref, sem_ref)
        _stage_weight_to_bf16(w_out_hbm, w_out_ref, stage_out_ref, sem_ref)
        h_odd_ref[...] = jnp.zeros(h_odd_ref.shape, F32)
        x_odd_ref[...] = jnp.zeros(x_odd_ref.shape, F32)
        ubuf_ref[...] = jnp.zeros(ubuf_ref.shape, F32)

    @pl.when(seq_tile == 0)
    def _():
        ubuf_ref[:, 0:2 * CONV_HALO, :] = jnp.zeros((ubuf_ref.shape[0], 2 * CONV_HALO, LANES), F32)
        state_ref[...] = jnp.zeros(state_ref.shape, F32)

    def step(h_next_ref, h_ref, x_keep_ref, x_ref):
        _mixer_step(h_next_ref, h_ref, x_keep_ref, x_ref, seq_tile, w_in_ref, w_out_ref, *io_refs, xb_ref,
                    ubuf_ref, cbuf_ref, state_ref)

    @pl.when(lax.rem(n, 2) == 0)
    def _():
        step(h_even_ref, h_odd_ref, x_even_ref, x_odd_ref)

    @pl.when(lax.rem(n, 2) == 1)
    def _():
        step(h_odd_ref, h_even_ref, x_odd_ref, x_even_ref)


def _mixer(x, w_in, conv_w, conv_b, cln_g, cln_b, gn_g, gn_b, w_out, ln1_g, ln1_b, tabs):
    B, S, D = x.shape
    T = MIXER_TILE
    tiles_per_seq = S // T
    n_tiles = B * tiles_per_seq

    def tile_index(tile):
        return lax.div(tile, tiles_per_seq), lax.rem(tile, tiles_per_seq)

    def next_tile(n):
        b, s = tile_index(jnp.minimum(n, n_tiles - 1))
        return b, s, 0

    def this_tile(n):
        b, s = tile_index(jnp.maximum(n - 1, 0))
        return b, s, 0

    const2 = lambda n: (0, 0)
    const3 = lambda n: (0, 0, 0)
    row = lambda width: pl.BlockSpec((1, width), const2)
    pos = lambda: pl.BlockSpec((S, RET_HEAD_DIM), const2, pipeline_mode=pl.Buffered(1))
    tab = lambda: pl.BlockSpec((RET_HEADS, RET_CHUNK, RET_HEAD_DIM), const3)
    return pl.pallas_call(
        functools.partial(_mixer_kernel, tiles_per_seq),
        grid=(n_tiles + 1,),
        in_specs=[
            pl.BlockSpec(memory_space=pl.ANY), pl.BlockSpec(memory_space=pl.ANY),
            pl.BlockSpec((1, T, D), next_tile),
            pl.BlockSpec((CONV_KERNEL, CONV_WIDTH), const2),
            row(CONV_WIDTH), row(CONV_WIDTH), row(CONV_WIDTH), row(RET_WIDTH), row(RET_WIDTH),
            row(D), row(D),
            pos(), pos(),
            tab(), tab(), tab(), tab(),
        ],
        out_specs=pl.BlockSpec((1, T, D), this_tile),
        out_shape=jax.ShapeDtypeStruct((B, S, D), F32),
        scratch_shapes=[
            pltpu.VMEM((N_PIECES, T, PROJ_COLS), F32),
            pltpu.VMEM((N_PIECES, T, PROJ_COLS), F32),
            pltpu.VMEM((T, D), F32),
            pltpu.VMEM((T, D), F32),
            pltpu.VMEM((T, D), BF16),
            pltpu.VMEM((N_STRIPS // 2, 2 * (CONV_HALO + T), LANES), F32),
            pltpu.VMEM((N_STRIPS, T, LANES), F32),
            pltpu.VMEM((RET_HEADS, RET_HEAD_DIM, RET_HEAD_DIM), F32),
            pltpu.VMEM((D, IN_COLS), BF16),
            pltpu.VMEM((D, D), BF16),
            pltpu.VMEM(_stage_shape(IN_COLS), F32),
            pltpu.VMEM(_stage_shape(D), F32),
            pltpu.SemaphoreType.DMA((2,)),
        ],
        compiler_params=pltpu.CompilerParams(dimension_semantics=("arbitrary",),
                                             vmem_limit_bytes=VMEM_LIMIT_BYTES),
        name="mixer",
    )(w_in, w_out, x, conv_w, conv_b, cln_g, cln_b, gn_g, gn_b, ln1_g, ln1_b, *tabs)


def _attn_mlp_kernel(x_ref, k_ref, v_ref, wq_hbm, wo_hbm, ln2_g_ref, ln2_b_ref, wup_hbm, wdn_hbm,
                     ln3_g_ref, ln3_b_ref, o_ref, wq_ref, wo_ref, wup_ref, wdn_ref, stage_d_ref, stage_ff_ref,
                     sem_ref):
    @pl.when((pl.program_id(0) == 0) & (pl.program_id(1) == 0))
    def _():
        _stage_weight_to_bf16(wq_hbm, wq_ref, stage_d_ref, sem_ref)
        _stage_weight_to_bf16(wo_hbm, wo_ref, stage_d_ref, sem_ref)
        _stage_weight_to_bf16(wup_hbm, wup_ref, stage_ff_ref, sem_ref)
        _stage_weight_to_bf16(wdn_hbm, wdn_ref, stage_d_ref, sem_ref)

    x = x_ref[0]
    T = x.shape[0]
    halves = (slice(0, T // 2), slice(T // 2, T))
    q = jnp.concatenate([(_dot(x[rows].astype(BF16), wq_ref[...]) * (XATTN_HEAD_DIM ** -0.5)).astype(BF16)
                         for rows in halves], axis=0)
    head_cols = [slice(h * XATTN_HEAD_DIM, (h + 1) * XATTN_HEAD_DIM) for h in range(XATTN_HEADS)]
    scores = [_dot_nt(q[:, hs], k_ref[0, :, hs]) for hs in head_cols]
    heads = []
    for s, hs in zip(scores, head_cols):
        p = jnp.exp(s - jnp.max(s, axis=-1, keepdims=True))
        l = jnp.sum(p, axis=-1, keepdims=True)
        o = _dot(p.astype(BF16), v_ref[0, :, hs])
        heads.append((o * (1.0 / l)).astype(BF16))
    attn = jnp.concatenate(heads, axis=1)
    xa = [_dot(attn[rows], wo_ref[...]) for rows in halves]
    x2 = [_layer_norm(DN_ALPHA * x[rows] + xa_r, ln2_g_ref[...], ln2_b_ref[...]) for rows, xa_r in zip(halves, xa)]
    x2b_halves = [x2_r.astype(BF16) for x2_r in x2]
    n_chunks = D_FF // FF_CHUNK
    up_cols = [slice(j * FF_CHUNK, (j + 1) * FF_CHUNK) for j in range(n_chunks)]

    def act(up):
        r = jnp.maximum(up, 0.0)
        return (r * r).astype(BF16)

    up = jnp.concatenate([_dot(x2b_r, wup_ref[:, up_cols[0]]) for x2b_r in x2b_halves], axis=0)
    x2b = jnp.concatenate(x2b_halves, axis=0)
    ff = None
    for j in range(1, n_chunks):
        up_next = _dot(x2b, wup_ref[:, up_cols[j]])
        part = _dot(act(up), wdn_ref[up_cols[j - 1], :])
        ff = part if ff is None else ff + part
        up = up_next
    hid = act(up)
    x2 = jnp.concatenate(x2, axis=0)
    for r0 in range(0, T, T // TAIL_SPLIT):
        rows = slice(r0, r0 + T // TAIL_SPLIT)
        ff_r = ff[rows] + _dot(hid[rows], wdn_ref[up_cols[-1], :])
        o_ref[0, rows, :] = _layer_norm(DN_ALPHA * x2[rows] + ff_r, ln3_g_ref[...], ln3_b_ref[...])


def _attn_mlp(x1, k, v, w_xq, w_xo, ln2_g, ln2_b, w_up, w_down, ln3_g, ln3_b):
    B, S, D = x1.shape
    T = ATTN_TILE
    const2 = lambda b, s: (0, 0)
    row = lambda n: pl.BlockSpec((1, n), const2)
    hbm = pl.BlockSpec(memory_space=pl.ANY)
    return pl.pallas_call(
        _attn_mlp_kernel,
        grid=(B, S // T),
        in_specs=[
            pl.BlockSpec((1, T, D), lambda b, s: (b, s, 0)),
            pl.BlockSpec((1, N_MEM, D), lambda b, s: (b, 0, 0)),
            pl.BlockSpec((1, N_MEM, D), lambda b, s: (b, 0, 0)),
            hbm, hbm,
            row(D), row(D),
            hbm, hbm,
            row(D), row(D),
        ],
        out_specs=pl.BlockSpec((1, T, D), lambda b, s: (b, s, 0)),
        out_shape=jax.ShapeDtypeStruct((B, S, D), F32),
        scratch_shapes=[
            pltpu.VMEM((D, D), BF16), pltpu.VMEM((D, D), BF16),
            pltpu.VMEM((D, D_FF), BF16), pltpu.VMEM((D_FF, D), BF16),
            pltpu.VMEM(_stage_shape(D), F32),
            pltpu.VMEM(_stage_shape(D_FF), F32),
            pltpu.SemaphoreType.DMA((2,)),
        ],
        compiler_params=pltpu.CompilerParams(dimension_semantics=("arbitrary", "arbitrary"),
                                             vmem_limit_bytes=VMEM_LIMIT_BYTES),
        name="attn_mlp",
    )(x1, k, v, w_xq, w_xo, ln2_g, ln2_b, w_up, w_down, ln3_g, ln3_b)


def _position_tables(seq):
    half = RET_HEAD_DIM // 2
    pos = jnp.arange(seq, dtype=F32)
    inv = ROPE_BASE ** (-jnp.linspace(0.0, 1.0, half, dtype=F32))
    ang = pos[:, None] * inv[None, :]
    cos, sin = jnp.cos(ang), jnp.sin(ang)
    cos2 = jnp.concatenate([cos, cos], axis=-1)
    sin2 = jnp.concatenate([-sin, sin], axis=-1)
    qs = RET_HEAD_DIM ** -0.5
    C = RET_CHUNK
    log_g = jnp.log(1.0 - 2.0 ** (-5.0 - jnp.arange(RET_HEADS, dtype=F32)))
    idx = jnp.arange(C, dtype=F32)
    rel = idx[:, None] - idx[None, :]
    decay = jnp.where(rel >= 0, jnp.exp(log_g[:, None, None] * jnp.maximum(rel, 0.0)), 0.0)
    q_dec = jnp.exp(log_g[:, None] * (idx + 1.0))
    k_dec = jnp.exp(log_g[:, None] * (C - 1.0 - idx))
    chunk_dec = jnp.exp(log_g * C)
    bcast = lambda t: jnp.broadcast_to(t, (RET_HEADS, C, RET_HEAD_DIM))
    return (cos2, sin2, decay * qs,
            bcast(q_dec[:, :, None]) * qs, bcast(k_dec[:, :, None]), bcast(chunk_dec[:, None, None]))


def kernel(x, mem, w_in, conv_w, conv_b, conv_ln_g, conv_ln_b, ret_gn_g, ret_gn_b, w_out,
           ln1_g, ln1_b, w_xq, w_xk, w_xv, w_xo, ln2_g, ln2_b, w_up, w_down, ln3_g, ln3_b):
    assert w_in.shape[0] == DEPTH == 1
    assert RET_CHUNK == RET_HEAD_DIM
    with jax.ensure_compile_time_eval():
        tabs = _position_tables(x.shape[1])
    k, v = _kv_proj(mem, w_xk[0], w_xv[0])
    x1 = _mixer(x, w_in[0], conv_w[0], conv_b, conv_ln_g, conv_ln_b, ret_gn_g, ret_gn_b, w_out[0],
                ln1_g, ln1_b, tabs)
    return _attn_mlp(x1, k, v, w_xq[0], w_xo[0], ln2_g, ln2_b, w_up[0], w_down[0], ln3_g, ln3_b)
```

```python
import functools

import jax
import jax.numpy as jnp
from jax import lax
from jax.experimental import pallas as pl
from jax.experimental.pallas import tpu as pltpu

D_MODEL = 1024
N_MEM = 256
CONV_WIDTH = 512
CONV_KERNEL = 31
RET_WIDTH = 512
RET_HEADS = 4
RET_HEAD_DIM = 128
RET_CHUNK = 128
ROPE_BASE = 10000.0
IN_COLS = 2 * CONV_WIDTH + 4 * RET_WIDTH
XATTN_HEADS = 4
XATTN_HEAD_DIM = 256
D_FF = 4 * D_MODEL
LN_EPS = 1e-5
DEPTH = 1
DN_ALPHA = (2.0 * DEPTH) ** 0.25

CONV_HALO = 32
SUBLANES, LANES = 8, 128
CONV_ROWS = 128
N_STRIPS = CONV_WIDTH // LANES
PROJ_COLS = 512
N_PIECES = IN_COLS // PROJ_COLS
P_A, P_B, P_Q, P_K, P_V, P_G = range(N_PIECES)
MIXER_TILE = 512
ATTN_TILE = 512
KV_BATCHES = 4
FF_CHUNK = 1024
TAIL_SPLIT = 4
VMEM_LIMIT_BYTES = 56 * 1024 * 1024
STAGE_BYTES = 1024 * 1024

BF16 = jnp.bfloat16
F32 = jnp.float32


def _dot(a, b):
    return jnp.dot(a, b, preferred_element_type=F32)


def _dot_nt(a, b):
    return lax.dot_general(a, b, (((1,), (1,)), ((), ())), preferred_element_type=F32)


def _sigmoid(x):
    return 1.0 / (1.0 + jnp.exp(-x))


def _layer_norm(x, g, b):
    mu = jnp.mean(x, axis=-1, keepdims=True)
    xc = x - mu
    var = jnp.mean(xc * xc, axis=-1, keepdims=True)
    return xc * lax.rsqrt(var + LN_EPS) * g + b


def _stage_shape(cols):
    rows = 1 << ((STAGE_BYTES // (4 * cols)).bit_length() - 1)
    return (2, rows, cols)


def _stage_weight_to_bf16(w_hbm_ref, w_bf16_ref, stage_ref, sem_ref):
    rows = stage_ref.shape[1]
    n_chunks = w_hbm_ref.shape[0] // rows
    assert n_chunks * rows == w_hbm_ref.shape[0] and stage_ref.shape[2] == w_hbm_ref.shape[1]

    def chunk_copy(i, slot):
        return pltpu.make_async_copy(w_hbm_ref.at[pl.ds(i * rows, rows)], stage_ref.at[slot], sem_ref.at[slot])

    chunk_copy(0, 0).start()

    def body(i, carry):
        slot = lax.rem(i, 2)

        @pl.when(i + 1 < n_chunks)
        def _():
            chunk_copy(i + 1, 1 - slot).start()

        chunk_copy(i, slot).wait()
        w_bf16_ref[pl.ds(pl.multiple_of(i * rows, rows), rows), :] = stage_ref[slot].astype(BF16)
        return carry

    lax.fori_loop(0, n_chunks, body, 0)


def _kv_kernel(mem_ref, wk_ref, wv_ref, k_ref, v_ref, wk_bf_ref, wv_bf_ref):
    @pl.when(pl.program_id(0) == 0)
    def _():
        wk_bf_ref[...] = wk_ref[...].astype(BF16)
        wv_bf_ref[...] = wv_ref[...].astype(BF16)

    m = mem_ref[...].reshape(KV_BATCHES * N_MEM, D_MODEL).astype(BF16)
    k_ref[...] = _dot(m, wk_bf_ref[...]).astype(BF16).reshape(k_ref.shape)
    v_ref[...] = _dot(m, wv_bf_ref[...]).astype(BF16).reshape(v_ref.shape)


def _kv_proj(mem, w_xk, w_xv):
    B = mem.shape[0]
    const = lambda b: (0, 0)
    batches = lambda: pl.BlockSpec((KV_BATCHES, N_MEM, D_MODEL), lambda b: (b, 0, 0))
    return pl.pallas_call(
        _kv_kernel,
        grid=(B // KV_BATCHES,),
        in_specs=[
            batches(),
            pl.BlockSpec((D_MODEL, D_MODEL), const),
            pl.BlockSpec((D_MODEL, D_MODEL), const),
        ],
        out_specs=[batches(), batches()],
        out_shape=[jax.ShapeDtypeStruct((B, N_MEM, D_MODEL), BF16)] * 2,
        scratch_shapes=[pltpu.VMEM((D_MODEL, D_MODEL), BF16)] * 2,
        compiler_params=pltpu.CompilerParams(dimension_semantics=("arbitrary",),
                                             vmem_limit_bytes=VMEM_LIMIT_BYTES),
        name="kv_proj",
    )(mem, w_xk, w_xv)


def _mixer_step(h_next_ref, h_ref, x_keep_ref, x_ref, seq_tile, w_in_ref, w_out_ref, x_next_ref, conv_w_ref,
                conv_b_ref, cln_g_ref, cln_b_ref, gn_g_ref, gn_b_ref, ln1_g_ref, ln1_b_ref, cos_ref, sin_ref,
                decay_ref, qdec_ref, kdec_ref, cdec_ref, o_ref, xb_ref, ubuf_ref, cbuf_ref, state_ref):
    T = MIXER_TILE

    def project(p):
        h_next_ref[p] = _dot(xb_ref[...], w_in_ref[:, p * PROJ_COLS:(p + 1) * PROJ_COLS])

    def strip_rows(slab_parity, t_start, n):
        slab, parity = slab_parity
        return slab, pl.ds(2 * t_start + parity, n, stride=2), slice(None)

    base = CONV_HALO - (CONV_KERNEL - 1)

    def conv_strip(c):
        where = divmod(c, 2)
        cols = slice(c * LANES, (c + 1) * LANES)
        n_acc = CONV_ROWS // SUBLANES
        for t0 in range(0, T, CONV_ROWS):
            acc = [jnp.broadcast_to(conv_b_ref[:, cols], (SUBLANES, LANES))] * n_acc
            for r in range(SUBLANES):
                taps = range(r, CONV_KERNEL, SUBLANES)
                wts = [jnp.broadcast_to(conv_w_ref[j:j + 1, cols], (SUBLANES, LANES)) for j in taps]
                for m in range(n_acc + len(taps) - 1):
                    win = ubuf_ref[strip_rows(where, t0 + base + r + SUBLANES * m, SUBLANES)]
                    for a, wt in enumerate(wts):
                        if 0 <= m - a < n_acc:
                            acc[m - a] = acc[m - a] + win * wt
            cbuf_ref[c, t0:t0 + CONV_ROWS, :] = jnp.concatenate(acc, axis=0)

    for c in range(N_STRIPS):
        pl.when(pl.program_id(0) >= 0)(functools.partial(conv_strip, c))
    ubuf_ref[:, 0:2 * CONV_HALO, :] = ubuf_ref[:, 2 * T:2 * (T + CONV_HALO), :]
    x_next = x_next_ref[0]
    xb_ref[...] = x_next.astype(BF16)
    project(P_A)
    x_keep_ref[...] = x_next
    project(P_B)
    u_next = h_next_ref[P_A] * _sigmoid(h_next_ref[P_B])
    for c in range(N_STRIPS):
        ubuf_ref[strip_rows(divmod(c, 2), CONV_HALO, T)] = u_next[:, c * LANES:(c + 1) * LANES]
    conv = jnp.concatenate([cbuf_ref[c] for c in range(N_STRIPS)], axis=1)
    cn = _layer_norm(conv, cln_g_ref[...], cln_b_ref[...])
    conv_out = (cn * _sigmoid(cn)).astype(BF16)
    mix_conv = _dot(conv_out, w_out_ref[0:CONV_WIDTH, :])

    pos_rows = pl.ds(pl.multiple_of(seq_tile * T, T), T)
    cos, sin = cos_ref[pos_rows, :], sin_ref[pos_rows, :]
    half = RET_HEAD_DIM // 2
    C = RET_CHUNK
    n_chunks = T // C
    head_cols = [slice(h * RET_HEAD_DIM, (h + 1) * RET_HEAD_DIM) for h in range(RET_HEADS)]
    q, v, scores, kv = [], [], [], []
    project(P_Q)
    for h, hs in enumerate(head_cols):
        qh = h_ref[P_Q, :, hs]
        kh = h_ref[P_K, :, hs]
        qh = qh * cos + pltpu.roll(qh, half, 1) * sin
        kh = kh * cos + pltpu.roll(kh, half, 1) * sin
        vh = h_ref[P_V, :, hs].astype(BF16)
        q.append(qh)
        v.append(vh)
        decay, kdec = decay_ref[h], kdec_ref[h]
        scores_h, kv_h = [], []
        for c in range(n_chunks):
            rows = slice(c * C, (c + 1) * C)
            scores_h.append((_dot_nt(qh[rows].astype(BF16), kh[rows].astype(BF16)) * decay).astype(BF16))
            kv_h.append(_dot((kh[rows] * kdec).T.astype(BF16), vh[rows]))
        scores.append(scores_h)
        kv.append(kv_h)
    project(P_K)
    ret_heads = []
    for h, hs in enumerate(head_cols):
        qdec, cdec = qdec_ref[h], cdec_ref[h]
        state = state_ref[h]
        ys = []
        for c in range(n_chunks):
            rows = slice(c * C, (c + 1) * C)
            lhs = jnp.concatenate([scores[h][c], (q[h][rows] * qdec).astype(BF16)], axis=1)
            rhs = jnp.concatenate([v[h][rows], state.astype(BF16)], axis=0)
            ys.append(_dot(lhs, rhs))
            state = state * cdec + kv[h][c]
        state_ref[h] = state
        y = jnp.concatenate(ys, axis=0)
        mu = jnp.mean(y, axis=-1, keepdims=True)
        yc = y - mu
        var = jnp.mean(yc * yc, axis=-1, keepdims=True)
        yn = yc * lax.rsqrt(var + LN_EPS)
        yn = yn * gn_g_ref[:, hs] + gn_b_ref[:, hs]
        gh = h_ref[P_G, :, hs]
        ret_heads.append((gh * _sigmoid(gh) * yn).astype(BF16))

    project(P_V)
    ret_out = jnp.concatenate(ret_heads, axis=1)
    mix = [mix_conv[r0:r0 + T // TAIL_SPLIT] + _dot(ret_out[r0:r0 + T // TAIL_SPLIT], w_out_ref[CONV_WIDTH:, :])
           for r0 in range(0, T, T // TAIL_SPLIT)]
    project(P_G)
    for i, mix_r in enumerate(mix):
        rows = slice(i * (T // TAIL_SPLIT), (i + 1) * (T // TAIL_SPLIT))
        o_ref[0, rows, :] = _layer_norm(DN_ALPHA * x_ref[rows, :] + mix_r, ln1_g_ref[...], ln1_b_ref[...])


def _mixer_kernel(tiles_per_seq, *refs):
    (w_in_hbm, w_out_hbm, *io_refs, h_even_ref, h_odd_ref, x_even_ref, x_odd_ref, xb_ref, ubuf_ref, cbuf_ref,
     state_ref, w_in_ref, w_out_ref, stage_in_ref, stage_out_ref, sem_ref) = refs
    n = pl.program_id(0)
    seq_tile = lax.rem(jnp.maximum(n - 1, 0), tiles_per_seq)

    @pl.when(n == 0)
    def _():
        _stage_weight_to_bf16(w_in_hbm, w_in_ref, stage_in_ref, sem_ref)
        _stage_weight_to_bf16(w_out_hbm, w_out_ref, stage_out_ref, sem_ref)
        h_odd_ref[...] = jnp.zeros(h_odd_ref.shape, F32)
        x_odd_ref[...] = jnp.zeros(x_odd_ref.shape, F32)
        ubuf_ref[...] = jnp.zeros(ubuf_ref.shape, F32)

    @pl.when(seq_tile == 0)
    def _():
        ubuf_ref[:, 0:2 * CONV_HALO, :] = jnp.zeros((ubuf_ref.shape[0], 2 * CONV_HALO, LANES), F32)
        state_ref[...] = jnp.zeros(state_ref.shape, F32)

    def step(h_next_ref, h_ref, x_keep_ref, x_ref):
        _mixer_step(h_next_ref, h_ref, x_keep_ref, x_ref, seq_tile, w_in_ref, w_out_ref, *io_refs, xb_ref,
                    ubuf_ref, cbuf_ref, state_ref)

    @pl.when(lax.rem(n, 2) == 0)
    def _():
        step(h_even_ref, h_odd_ref, x_even_ref, x_odd_ref)

    @pl.when(lax.rem(n, 2) == 1)
    def _():
        step(h_odd_ref, h_even_ref, x_odd_ref, x_even_ref)


def _mixer(x, w_in, conv_w, conv_b, cln_g, cln_b, gn_g, gn_b, w_out, ln1_g, ln1_b, tabs):
    B, S, D = x.shape
    T = MIXER_TILE
    tiles_per_seq = S // T
    n_tiles = B * tiles_per_seq

    def tile_index(tile):
        return lax.div(tile, tiles_per_seq), lax.rem(tile, tiles_per_seq)

    def next_tile(n):
        b, s = tile_index(jnp.minimum(n, n_tiles - 1))
        return b, s, 0

    def this_tile(n):
        b, s = tile_index(jnp.maximum(n - 1, 0))
        return b, s, 0

    const2 = lambda n: (0, 0)
    const3 = lambda n: (0, 0, 0)
    row = lambda width: pl.BlockSpec((1, width), const2)
    pos = lambda: pl.BlockSpec((S, RET_HEAD_DIM), const2, pipeline_mode=pl.Buffered(1))
    tab = lambda: pl.BlockSpec((RET_HEADS, RET_CHUNK, RET_HEAD_DIM), const3)
    return pl.pallas_call(
        functools.partial(_mixer_kernel, tiles_per_seq),
        grid=(n_tiles + 1,),
        in_specs=[
            pl.BlockSpec(memory_space=pl.ANY), pl.BlockSpec(memory_space=pl.ANY),
            pl.BlockSpec((1, T, D), next_tile),
            pl.BlockSpec((CONV_KERNEL, CONV_WIDTH), const2),
            row(CONV_WIDTH), row(CONV_WIDTH), row(CONV_WIDTH), row(RET_WIDTH), row(RET_WIDTH),
            row(D), row(D),
            pos(), pos(),
            tab(), tab(), tab(), tab(),
        ],
        out_specs=pl.BlockSpec((1, T, D), this_tile),
        out_shape=jax.ShapeDtypeStruct((B, S, D), F32),
        scratch_shapes=[
            pltpu.VMEM((N_PIECES, T, PROJ_COLS), F32),
            pltpu.VMEM((N_PIECES, T, PROJ_COLS), F32),
            pltpu.VMEM((T, D), F32),
            pltpu.VMEM((T, D), F32),
            pltpu.VMEM((T, D), BF16),
            pltpu.VMEM((N_STRIPS // 2, 2 * (CONV_HALO + T), LANES), F32),
            pltpu.VMEM((N_STRIPS, T, LANES), F32),
            pltpu.VMEM((RET_HEADS, RET_HEAD_DIM, RET_HEAD_DIM), F32),
            pltpu.VMEM((D, IN_COLS), BF16),
            pltpu.VMEM((D, D), BF16),
            pltpu.VMEM(_stage_shape(IN_COLS), F32),
            pltpu.VMEM(_stage_shape(D), F32),
            pltpu.SemaphoreType.DMA((2,)),
        ],
        compiler_params=pltpu.CompilerParams(dimension_semantics=("arbitrary",),
                                             vmem_limit_bytes=VMEM_LIMIT_BYTES),
        name="mixer",
    )(w_in, w_out, x, conv_w, conv_b, cln_g, cln_b, gn_g, gn_b, ln1_g, ln1_b, *tabs)


def _attn_mlp_kernel(x_ref, k_ref, v_ref, wq_hbm, wo_hbm, ln2_g_ref, ln2_b_ref, wup_hbm, wdn_hbm,
                     ln3_g_ref, ln3_b_ref, o_ref, wq_ref, wo_ref, wup_ref, wdn_ref, stage_d_ref, stage_ff_ref,
                     sem_ref):
    @pl.when((pl.program_id(0) == 0) & (pl.program_id(1) == 0))
    def _():
        _stage_weight_to_bf16(wq_hbm, wq_ref, stage_d_ref, sem_ref)
        _stage_weight_to_bf16(wo_hbm, wo_ref, stage_d_ref, sem_ref)
        _stage_weight_to_bf16(wup_hbm, wup_ref, stage_ff_ref, sem_ref)
        _stage_weight_to_bf16(wdn_hbm, wdn_ref, stage_d_ref, sem_ref)

    x = x_ref[0]
    T = x.shape[0]
    halves = (slice(0, T // 2), slice(T // 2, T))
    q = (_dot(x.astype(BF16), wq_ref[...]) * (XATTN_HEAD_DIM ** -0.5)).astype(BF16)
    head_cols = [slice(h * XATTN_HEAD_DIM, (h + 1) * XATTN_HEAD_DIM) for h in range(XATTN_HEADS)]
    scores = [_dot_nt(q[:, hs], k_ref[0, :, hs]) for hs in head_cols]
    heads = []
    for s, hs in zip(scores, head_cols):
        p = jnp.exp(s - jnp.max(s, axis=-1, keepdims=True))
        l = jnp.sum(p, axis=-1, keepdims=True)
        o = _dot(p.astype(BF16), v_ref[0, :, hs])
        heads.append((o * (1.0 / l)).astype(BF16))
    attn = jnp.concatenate(heads, axis=1)
    xa = [_dot(attn[rows], wo_ref[...]) for rows in halves]
    x2 = [_layer_norm(DN_ALPHA * x[rows] + xa_r, ln2_g_ref[...], ln2_b_ref[...]) for rows, xa_r in zip(halves, xa)]
    x2b_halves = [x2_r.astype(BF16) for x2_r in x2]
    n_chunks = D_FF // FF_CHUNK
    up_cols = [slice(j * FF_CHUNK, (j + 1) * FF_CHUNK) for j in range(n_chunks)]

    def act(up):
        r = jnp.maximum(up, 0.0)
        return (r * r).astype(BF16)

    up = jnp.concatenate([_dot(x2b_r, wup_ref[:, up_cols[0]]) for x2b_r in x2b_halves], axis=0)
    x2b = jnp.concatenate(x2b_halves, axis=0)
    ff = None
    for j in range(1, n_chunks):
        up_next = _dot(x2b, wup_ref[:, up_cols[j]])
        part = _dot(act(up), wdn_ref[up_cols[j - 1], :])
        ff = part if ff is None else ff + part
        up = up_next
    hid = act(up)
    for rows, x2_r in zip(halves, x2):
        ff_r = ff[rows] + _dot(hid[rows], wdn_ref[up_cols[-1], :])
        o_ref[0, rows, :] = _layer_norm(DN_ALPHA * x2_r + ff_r, ln3_g_ref[...], ln3_b_ref[...])


def _attn_mlp(x1, k, v, w_xq, w_xo, ln2_g, ln2_b, w_up, w_down, ln3_g, ln3_b):
    B, S, D = x1.shape
    T = ATTN_TILE
    const2 = lambda b, s: (0, 0)
    row = lambda n: pl.BlockSpec((1, n), const2)
    hbm = pl.BlockSpec(memory_space=pl.ANY)
    return pl.pallas_call(
        _attn_mlp_kernel,
        grid=(B, S // T),
        in_specs=[
            pl.BlockSpec((1, T, D), lambda b, s: (b, s, 0)),
            pl.BlockSpec((1, N_MEM, D), lambda b, s: (b, 0, 0)),
            pl.BlockSpec((1, N_MEM, D), lambda b, s: (b, 0, 0)),
            hbm, hbm,
            row(D), row(D),
            hbm, hbm,
            row(D), row(D),
        ],
        out_specs=pl.BlockSpec((1, T, D), lambda b, s: (b, s, 0)),
        out_shape=jax.ShapeDtypeStruct((B, S, D), F32),
        scratch_shapes=[
            pltpu.VMEM((D, D), BF16), pltpu.VMEM((D, D), BF16),
            pltpu.VMEM((D, D_FF), BF16), pltpu.VMEM((D_FF, D), BF16),
            pltpu.VMEM(_stage_shape(D), F32),
            pltpu.VMEM(_stage_shape(D_FF), F32),
            pltpu.SemaphoreType.DMA((2,)),
        ],
        compiler_params=pltpu.CompilerParams(dimension_semantics=("arbitrary", "arbitrary"),
                                             vmem_limit_bytes=VMEM_LIMIT_BYTES),
        name="attn_mlp",
    )(x1, k, v, w_xq, w_xo, ln2_g, ln2_b, w_up, w_down, ln3_g, ln3_b)


def _position_tables(seq):
    half = RET_HEAD_DIM // 2
    pos = jnp.arange(seq, dtype=F32)
    inv = ROPE_BASE ** (-jnp.linspace(0.0, 1.0, half, dtype=F32))
    ang = pos[:, None] * inv[None, :]
    cos, sin = jnp.cos(ang), jnp.sin(ang)
    cos2 = jnp.concatenate([cos, cos], axis=-1)
    sin2 = jnp.concatenate([-sin, sin], axis=-1)
    qs = RET_HEAD_DIM ** -0.5
    C = RET_CHUNK
    log_g = jnp.log(1.0 - 2.0 ** (-5.0 - jnp.arange(RET_HEADS, dtype=F32)))
    idx = jnp.arange(C, dtype=F32)
    rel = idx[:, None] - idx[None, :]
    decay = jnp.where(rel >= 0, jnp.exp(log_g[:, None, None] * jnp.maximum(rel, 0.0)), 0.0)
    q_dec = jnp.exp(log_g[:, None] * (idx + 1.0))
    k_dec = jnp.exp(log_g[:, None] * (C - 1.0 - idx))
    chunk_dec = jnp.exp(log_g * C)
    bcast = lambda t: jnp.broadcast_to(t, (RET_HEADS, C, RET_HEAD_DIM))
    return (cos2, sin2, decay * qs,
            bcast(q_dec[:, :, None]) * qs, bcast(k_dec[:, :, None]), bcast(chunk_dec[:, None, None]))


def kernel(x, mem, w_in, conv_w, conv_b, conv_ln_g, conv_ln_b, ret_gn_g, ret_gn_b, w_out,
           ln1_g, ln1_b, w_xq, w_xk, w_xv, w_xo, ln2_g, ln2_b, w_up, w_down, ln3_g, ln3_b):
    assert w_in.shape[0] == DEPTH == 1
    assert RET_CHUNK == RET_HEAD_DIM
    with jax.ensure_compile_time_eval():
        tabs = _position_tables(x.shape[1])
    k, v = _kv_proj(mem, w_xk[0], w_xv[0])
    x1 = _mixer(x, w_in[0], conv_w[0], conv_b, conv_ln_g, conv_ln_b, ret_gn_g, ret_gn_b, w_out[0],
                ln1_g, ln1_b, tabs)
    return _attn_mlp(x1, k, v, w_xq[0], w_xo[0], ln2_g, ln2_b, w_up[0], w_down[0], ln3_g, ln3_b)
```

```python
import functools

import jax
import jax.numpy as jnp
from jax import lax
from jax.experimental import pallas as pl
from jax.experimental.pallas import tpu as pltpu

D_MODEL = 1024
N_MEM = 256
CONV_WIDTH = 512
CONV_KERNEL = 31
RET_WIDTH = 512
RET_HEADS = 4
RET_HEAD_DIM = 128
RET_CHUNK = 128
ROPE_BASE = 10000.0
IN_COLS = 2 * CONV_WIDTH + 4 * RET_WIDTH
XATTN_HEADS = 4
XATTN_HEAD_DIM = 256
D_FF = 4 * D_MODEL
LN_EPS = 1e-5
DEPTH = 1
DN_ALPHA = (2.0 * DEPTH) ** 0.25

CONV_HALO = 32
SUBLANES, LANES = 8, 128
CONV_ROWS = 128
N_STRIPS = CONV_WIDTH // LANES
PROJ_COLS = 512
N_PIECES = IN_COLS // PROJ_COLS
P_A, P_B, P_Q, P_K, P_V, P_G = range(N_PIECES)
MIXER_TILE = 512
ATTN_TILE = 512
KV_BATCHES = 4
FF_CHUNK = 1024
TAIL_SPLIT = 2
VMEM_LIMIT_BYTES = 56 * 1024 * 1024
STAGE_BYTES = 1024 * 1024

BF16 = jnp.bfloat16
F32 = jnp.float32


def _dot(a, b):
    return jnp.dot(a, b, preferred_element_type=F32)


def _dot_nt(a, b):
    return lax.dot_general(a, b, (((1,), (1,)), ((), ())), preferred_element_type=F32)


def _sigmoid(x):
    return 1.0 / (1.0 + jnp.exp(-x))


def _layer_norm(x, g, b):
    mu = jnp.mean(x, axis=-1, keepdims=True)
    xc = x - mu
    var = jnp.mean(xc * xc, axis=-1, keepdims=True)
    return xc * lax.rsqrt(var + LN_EPS) * g + b


def _stage_shape(cols):
    rows = 1 << ((STAGE_BYTES // (4 * cols)).bit_length() - 1)
    return (2, rows, cols)


def _stage_weight_to_bf16(w_hbm_ref, w_bf16_ref, stage_ref, sem_ref):
    rows = stage_ref.shape[1]
    n_chunks = w_hbm_ref.shape[0] // rows
    assert n_chunks * rows == w_hbm_ref.shape[0] and stage_ref.shape[2] == w_hbm_ref.shape[1]

    def chunk_copy(i, slot):
        return pltpu.make_async_copy(w_hbm_ref.at[pl.ds(i * rows, rows)], stage_ref.at[slot], sem_ref.at[slot])

    chunk_copy(0, 0).start()

    def body(i, carry):
        slot = lax.rem(i, 2)

        @pl.when(i + 1 < n_chunks)
        def _():
            chunk_copy(i + 1, 1 - slot).start()

        chunk_copy(i, slot).wait()
        w_bf16_ref[pl.ds(pl.multiple_of(i * rows, rows), rows), :] = stage_ref[slot].astype(BF16)
        return carry

    lax.fori_loop(0, n_chunks, body, 0)


def _kv_kernel(mem_ref, wk_ref, wv_ref, k_ref, v_ref, wk_bf_ref, wv_bf_ref):
    @pl.when(pl.program_id(0) == 0)
    def _():
        wk_bf_ref[...] = wk_ref[...].astype(BF16)
        wv_bf_ref[...] = wv_ref[...].astype(BF16)

    m = mem_ref[...].reshape(KV_BATCHES * N_MEM, D_MODEL).astype(BF16)
    k_ref[...] = _dot(m, wk_bf_ref[...]).astype(BF16).reshape(k_ref.shape)
    v_ref[...] = _dot(m, wv_bf_ref[...]).astype(BF16).reshape(v_ref.shape)


def _kv_proj(mem, w_xk, w_xv):
    B = mem.shape[0]
    const = lambda b: (0, 0)
    batches = lambda: pl.BlockSpec((KV_BATCHES, N_MEM, D_MODEL), lambda b: (b, 0, 0))
    return pl.pallas_call(
        _kv_kernel,
        grid=(B // KV_BATCHES,),
        in_specs=[
            batches(),
            pl.BlockSpec((D_MODEL, D_MODEL), const),
            pl.BlockSpec((D_MODEL, D_MODEL), const),
        ],
        out_specs=[batches(), batches()],
        out_shape=[jax.ShapeDtypeStruct((B, N_MEM, D_MODEL), BF16)] * 2,
        scratch_shapes=[pltpu.VMEM((D_MODEL, D_MODEL), BF16)] * 2,
        compiler_params=pltpu.CompilerParams(dimension_semantics=("arbitrary",),
                                             vmem_limit_bytes=VMEM_LIMIT_BYTES),
        name="kv_proj",
    )(mem, w_xk, w_xv)


def _mixer_step(h_next_ref, h_ref, x_keep_ref, x_ref, seq_tile, w_in_ref, w_out_ref, x_next_ref, conv_w_ref,
                conv_b_ref, cln_g_ref, cln_b_ref, gn_g_ref, gn_b_ref, ln1_g_ref, ln1_b_ref, cos_ref, sin_ref,
                decay_ref, qdec_ref, kdec_ref, cdec_ref, o_ref, xb_ref, ubuf_ref, cbuf_ref, state_ref):
    T = MIXER_TILE

    def project(p):
        h_next_ref[p] = _dot(xb_ref[...], w_in_ref[:, p * PROJ_COLS:(p + 1) * PROJ_COLS])

    def strip_rows(slab_parity, t_start, n):
        slab, parity = slab_parity
        return slab, pl.ds(2 * t_start + parity, n, stride=2), slice(None)

    base = CONV_HALO - (CONV_KERNEL - 1)

    def conv_strip(c):
        where = divmod(c, 2)
        cols = slice(c * LANES, (c + 1) * LANES)
        n_acc = CONV_ROWS // SUBLANES
        for t0 in range(0, T, CONV_ROWS):
            acc = [jnp.broadcast_to(conv_b_ref[:, cols], (SUBLANES, LANES))] * n_acc
            for r in range(SUBLANES):
                taps = range(r, CONV_KERNEL, SUBLANES)
                wts = [jnp.broadcast_to(conv_w_ref[j:j + 1, cols], (SUBLANES, LANES)) for j in taps]
                for m in range(n_acc + len(taps) - 1):
                    win = ubuf_ref[strip_rows(where, t0 + base + r + SUBLANES * m, SUBLANES)]
                    for a, wt in enumerate(wts):
                        if 0 <= m - a < n_acc:
                            acc[m - a] = acc[m - a] + win * wt
            cbuf_ref[c, t0:t0 + CONV_ROWS, :] = jnp.concatenate(acc, axis=0)

    for c in range(N_STRIPS):
        pl.when(pl.program_id(0) >= 0)(functools.partial(conv_strip, c))
    ubuf_ref[:, 0:2 * CONV_HALO, :] = ubuf_ref[:, 2 * T:2 * (T + CONV_HALO), :]
    x_next = x_next_ref[0]
    xb_ref[...] = x_next.astype(BF16)
    project(P_A)
    x_keep_ref[...] = x_next
    project(P_B)
    u_next = h_next_ref[P_A] * _sigmoid(h_next_ref[P_B])
    for c in range(N_STRIPS):
        ubuf_ref[strip_rows(divmod(c, 2), CONV_HALO, T)] = u_next[:, c * LANES:(c + 1) * LANES]
    conv = jnp.concatenate([cbuf_ref[c] for c in range(N_STRIPS)], axis=1)
    cn = _layer_norm(conv, cln_g_ref[...], cln_b_ref[...])
    conv_out = (cn * _sigmoid(cn)).astype(BF16)
    mix_conv = _dot(conv_out, w_out_ref[0:CONV_WIDTH, :])

    pos_rows = pl.ds(pl.multiple_of(seq_tile * T, T), T)
    cos, sin = cos_ref[pos_rows, :], sin_ref[pos_rows, :]
    half = RET_HEAD_DIM // 2
    C = RET_CHUNK
    n_chunks = T // C
    head_cols = [slice(h * RET_HEAD_DIM, (h + 1) * RET_HEAD_DIM) for h in range(RET_HEADS)]
    q, v, scores, kv = [], [], [], []
    project(P_Q)
    for h, hs in enumerate(head_cols):
        qh = h_ref[P_Q, :, hs]
        kh = h_ref[P_K, :, hs]
        qh = qh * cos + pltpu.roll(qh, half, 1) * sin
        kh = kh * cos + pltpu.roll(kh, half, 1) * sin
        vh = h_ref[P_V, :, hs].astype(BF16)
        q.append(qh)
        v.append(vh)
        decay, kdec = decay_ref[h], kdec_ref[h]
        scores_h, kv_h = [], []
        for c in range(n_chunks):
            rows = slice(c * C, (c + 1) * C)
            scores_h.append((_dot_nt(qh[rows].astype(BF16), kh[rows].astype(BF16)) * decay).astype(BF16))
            kv_h.append(_dot((kh[rows] * kdec).T.astype(BF16), vh[rows]))
        scores.append(scores_h)
        kv.append(kv_h)
    project(P_K)
    ret_heads = []
    for h, hs in enumerate(head_cols):
        qdec, cdec = qdec_ref[h], cdec_ref[h]
        state = state_ref[h]
        ys = []
        for c in range(n_chunks):
            rows = slice(c * C, (c + 1) * C)
            lhs = jnp.concatenate([scores[h][c], (q[h][rows] * qdec).astype(BF16)], axis=1)
            rhs = jnp.concatenate([v[h][rows], state.astype(BF16)], axis=0)
            ys.append(_dot(lhs, rhs))
            state = state * cdec + kv[h][c]
        state_ref[h] = state
        y = jnp.concatenate(ys, axis=0)
        mu = jnp.mean(y, axis=-1, keepdims=True)
        yc = y - mu
        var = jnp.mean(yc * yc, axis=-1, keepdims=True)
        yn = yc * lax.rsqrt(var + LN_EPS)
        yn = yn * gn_g_ref[:, hs] + gn_b_ref[:, hs]
        gh = h_ref[P_G, :, hs]
        ret_heads.append((gh * _sigmoid(gh) * yn).astype(BF16))

    project(P_V)
    ret_out = jnp.concatenate(ret_heads, axis=1)
    mix = [mix_conv[r0:r0 + T // TAIL_SPLIT] + _dot(ret_out[r0:r0 + T // TAIL_SPLIT], w_out_ref[CONV_WIDTH:, :])
           for r0 in range(0, T, T // TAIL_SPLIT)]
    project(P_G)
    for i, mix_r in enumerate(mix):
        rows = slice(i * (T // TAIL_SPLIT), (i + 1) * (T // TAIL_SPLIT))
        o_ref[0, rows, :] = _layer_norm(DN_ALPHA * x_ref[rows, :] + mix_r, ln1_g_ref[...], ln1_b_ref[...])


def _mixer_kernel(tiles_per_seq, *refs):
    (w_in_hbm, w_out_hbm, *io_refs, h_even_ref, h_odd_ref, x_even_ref, x_odd_ref, xb_ref, ubuf_ref, cbuf_ref,
     state_ref, w_in_ref, w_out_ref, stage_in_ref, stage_out_ref, sem_ref) = refs
    n = pl.program_id(0)
    seq_tile = lax.rem(jnp.maximum(n - 1, 0), tiles_per_seq)

    @pl.when(n == 0)
    def _():
        _stage_weight_to_bf16(w_in_hbm, w_in_ref, stage_in_ref, sem_ref)
        _stage_weight_to_bf16(w_out_hbm, w_out_ref, stage_out_ref, sem_ref)
        h_odd_ref[...] = jnp.zeros(h_odd_ref.shape, F32)
        x_odd_ref[...] = jnp.zeros(x_odd_ref.shape, F32)
        ubuf_ref[...] = jnp.zeros(ubuf_ref.shape, F32)

    @pl.when(seq_tile == 0)
    def _():
        ubuf_ref[:, 0:2 * CONV_HALO, :] = jnp.zeros((ubuf_ref.shape[0], 2 * CONV_HALO, LANES), F32)
        state_ref[...] = jnp.zeros(state_ref.shape, F32)

    def step(h_next_ref, h_ref, x_keep_ref, x_ref):
        _mixer_step(h_next_ref, h_ref, x_keep_ref, x_ref, seq_tile, w_in_ref, w_out_ref, *io_refs, xb_ref,
                    ubuf_ref, cbuf_ref, state_ref)

    @pl.when(lax.rem(n, 2) == 0)
    def _():
        step(h_even_ref, h_odd_ref, x_even_ref, x_odd_ref)

    @pl.when(lax.rem(n, 2) == 1)
    def _():
        step(h_odd_ref, h_even_ref, x_odd_ref, x_even_ref)


def _mixer(x, w_in, conv_w, conv_b, cln_g, cln_b, gn_g, gn_b, w_out, ln1_g, ln1_b, tabs):
    B, S, D = x.shape
    T = MIXER_TILE
    tiles_per_seq = S // T
    n_tiles = B * tiles_per_seq

    def tile_index(tile):
        return lax.div(tile, tiles_per_seq), lax.rem(tile, tiles_per_seq)

    def next_tile(n):
        b, s = tile_index(jnp.minimum(n, n_tiles - 1))
        return b, s, 0

    def this_tile(n):
        b, s = tile_index(jnp.maximum(n - 1, 0))
        return b, s, 0

    const2 = lambda n: (0, 0)
    const3 = lambda n: (0, 0, 0)
    row = lambda width: pl.BlockSpec((1, width), const2)
    pos = lambda: pl.BlockSpec((S, RET_HEAD_DIM), const2, pipeline_mode=pl.Buffered(1))
    tab = lambda: pl.BlockSpec((RET_HEADS, RET_CHUNK, RET_HEAD_DIM), const3)
    return pl.pallas_call(
        functools.partial(_mixer_kernel, tiles_per_seq),
        grid=(n_tiles + 1,),
        in_specs=[
            pl.BlockSpec(memory_space=pl.ANY), pl.BlockSpec(memory_space=pl.ANY),
            pl.BlockSpec((1, T, D), next_tile),
            pl.BlockSpec((CONV_KERNEL, CONV_WIDTH), const2),
            row(CONV_WIDTH), row(CONV_WIDTH), row(CONV_WIDTH), row(RET_WIDTH), row(RET_WIDTH),
            row(D), row(D),
            pos(), pos(),
            tab(), tab(), tab(), tab(),
        ],
        out_specs=pl.BlockSpec((1, T, D), this_tile),
        out_shape=jax.ShapeDtypeStruct((B, S, D), F32),
        scratch_shapes=[
            pltpu.VMEM((N_PIECES, T, PROJ_COLS), F32),
            pltpu.VMEM((N_PIECES, T, PROJ_COLS), F32),
            pltpu.VMEM((T, D), F32),
            pltpu.VMEM((T, D), F32),
            pltpu.VMEM((T, D), BF16),
            pltpu.VMEM((N_STRIPS // 2, 2 * (CONV_HALO + T), LANES), F32),
            pltpu.VMEM((N_STRIPS, T, LANES), F32),
            pltpu.VMEM((RET_HEADS, RET_HEAD_DIM, RET_HEAD_DIM), F32),
            pltpu.VMEM((D, IN_COLS), BF16),
            pltpu.VMEM((D, D), BF16),
            pltpu.VMEM(_stage_shape(IN_COLS), F32),
            pltpu.VMEM(_stage_shape(D), F32),
            pltpu.SemaphoreType.DMA((2,)),
        ],
        compiler_params=pltpu.CompilerParams(dimension_semantics=("arbitrary",),
                                             vmem_limit_bytes=VMEM_LIMIT_BYTES),
        name="mixer",
    )(w_in, w_out, x, conv_w, conv_b, cln_g, cln_b, gn_g, gn_b, ln1_g, ln1_b, *tabs)


def _attn_mlp_kernel(x_ref, k_ref, v_ref, wq_hbm, wo_hbm, ln2_g_ref, ln2_b_ref, wup_hbm, wdn_hbm,
                     ln3_g_ref, ln3_b_ref, o_ref, wq_ref, wo_ref, wup_ref, wdn_ref, stage_d_ref, stage_ff_ref,
                     sem_ref):
    @pl.when((pl.program_id(0) == 0) & (pl.program_id(1) == 0))
    def _():
        _stage_weight_to_bf16(wq_hbm, wq_ref, stage_d_ref, sem_ref)
        _stage_weight_to_bf16(wo_hbm, wo_ref, stage_d_ref, sem_ref)
        _stage_weight_to_bf16(wup_hbm, wup_ref, stage_ff_ref, sem_ref)
        _stage_weight_to_bf16(wdn_hbm, wdn_ref, stage_d_ref, sem_ref)

    x = x_ref[0]
    T = x.shape[0]
    halves = (slice(0, T // 2), slice(T // 2, T))
    q = (_dot(x.astype(BF16), wq_ref[...]) * (XATTN_HEAD_DIM ** -0.5)).astype(BF16)
    head_cols = [slice(h * XATTN_HEAD_DIM, (h + 1) * XATTN_HEAD_DIM) for h in range(XATTN_HEADS)]
    scores = [_dot_nt(q[:, hs], k_ref[0, :, hs]) for hs in head_cols]
    heads = []
    for s, hs in zip(scores, head_cols):
        p = jnp.exp(s - jnp.max(s, axis=-1, keepdims=True))
        l = jnp.sum(p, axis=-1, keepdims=True)
        o = _dot(p.astype(BF16), v_ref[0, :, hs])
        heads.append((o * (1.0 / l)).astype(BF16))
    attn = jnp.concatenate(heads, axis=1)
    xa = [_dot(attn[rows], wo_ref[...]) for rows in halves]
    x2 = [_layer_norm(DN_ALPHA * x[rows] + xa_r, ln2_g_ref[...], ln2_b_ref[...]) for rows, xa_r in zip(halves, xa)]
    x2b_halves = [x2_r.astype(BF16) for x2_r in x2]
    n_chunks = D_FF // FF_CHUNK
    up_cols = [slice(j * FF_CHUNK, (j + 1) * FF_CHUNK) for j in range(n_chunks)]

    def act(up):
        r = jnp.maximum(up, 0.0)
        return (r * r).astype(BF16)

    up = jnp.concatenate([_dot(x2b_r, wup_ref[:, up_cols[0]]) for x2b_r in x2b_halves], axis=0)
    x2b = jnp.concatenate(x2b_halves, axis=0)
    ff = None
    for j in range(1, n_chunks):
        up_next = _dot(x2b, wup_ref[:, up_cols[j]])
        part = _dot(act(up), wdn_ref[up_cols[j - 1], :])
        ff = part if ff is None else ff + part
        up = up_next
    hid = act(up)
    for rows, x2_r in zip(halves, x2):
        ff_r = ff[rows] + _dot(hid[rows], wdn_ref[up_cols[-1], :])
        o_ref[0, rows, :] = _layer_norm(DN_ALPHA * x2_r + ff_r, ln3_g_ref[...], ln3_b_ref[...])


def _attn_mlp(x1, k, v, w_xq, w_xo, ln2_g, ln2_b, w_up, w_down, ln3_g, ln3_b):
    B, S, D = x1.shape
    T = ATTN_TILE
    const2 = lambda b, s: (0, 0)
    row = lambda n: pl.BlockSpec((1, n), const2)
    hbm = pl.BlockSpec(memory_space=pl.ANY)
    return pl.pallas_call(
        _attn_mlp_kernel,
        grid=(B, S // T),
        in_specs=[
            pl.BlockSpec((1, T, D), lambda b, s: (b, s, 0)),
            pl.BlockSpec((1, N_MEM, D), lambda b, s: (b, 0, 0)),
            pl.BlockSpec((1, N_MEM, D), lambda b, s: (b, 0, 0)),
            hbm, hbm,
            row(D), row(D),
            hbm, hbm,
            row(D), row(D),
        ],
        out_specs=pl.BlockSpec((1, T, D), lambda b, s: (b, s, 0)),
        out_shape=jax.ShapeDtypeStruct((B, S, D), F32),
        scratch_shapes=[
            pltpu.VMEM((D, D), BF16), pltpu.VMEM((D, D), BF16),
            pltpu.VMEM((D, D_FF), BF16), pltpu.VMEM((D_FF, D), BF16),
            pltpu.VMEM(_stage_shape(D), F32),
            pltpu.VMEM(_stage_shape(D_FF), F32),
            pltpu.SemaphoreType.DMA((2,)),
        ],
        compiler_params=pltpu.CompilerParams(dimension_semantics=("arbitrary", "arbitrary"),
                                             vmem_limit_bytes=VMEM_LIMIT_BYTES),
        name="attn_mlp",
    )(x1, k, v, w_xq, w_xo, ln2_g, ln2_b, w_up, w_down, ln3_g, ln3_b)


def _position_tables(seq):
    half = RET_HEAD_DIM // 2
    pos = jnp.arange(seq, dtype=F32)
    inv = ROPE_BASE ** (-jnp.linspace(0.0, 1.0, half, dtype=F32))
    ang = pos[:, None] * inv[None, :]
    cos, sin = jnp.cos(ang), jnp.sin(ang)
    cos2 = jnp.concatenate([cos, cos], axis=-1)
    sin2 = jnp.concatenate([-sin, sin], axis=-1)
    qs = RET_HEAD_DIM ** -0.5
    C = RET_CHUNK
    log_g = jnp.log(1.0 - 2.0 ** (-5.0 - jnp.arange(RET_HEADS, dtype=F32)))
    idx = jnp.arange(C, dtype=F32)
    rel = idx[:, None] - idx[None, :]
    decay = jnp.where(rel >= 0, jnp.exp(log_g[:, None, None] * jnp.maximum(rel, 0.0)), 0.0)
    q_dec = jnp.exp(log_g[:, None] * (idx + 1.0))
    k_dec = jnp.exp(log_g[:, None] * (C - 1.0 - idx))
    chunk_dec = jnp.exp(log_g * C)
    bcast = lambda t: jnp.broadcast_to(t, (RET_HEADS, C, RET_HEAD_DIM))
    return (cos2, sin2, decay * qs,
            bcast(q_dec[:, :, None]) * qs, bcast(k_dec[:, :, None]), bcast(chunk_dec[:, None, None]))


def kernel(x, mem, w_in, conv_w, conv_b, conv_ln_g, conv_ln_b, ret_gn_g, ret_gn_b, w_out,
           ln1_g, ln1_b, w_xq, w_xk, w_xv, w_xo, ln2_g, ln2_b, w_up, w_down, ln3_g, ln3_b):
    assert w_in.shape[0] == DEPTH == 1
    assert RET_CHUNK == RET_HEAD_DIM
    with jax.ensure_compile_time_eval():
        tabs = _position_tables(x.shape[1])
    k, v = _kv_proj(mem, w_xk[0], w_xv[0])
    x1 = _mixer(x, w_in[0], conv_w[0], conv_b, conv_ln_g, conv_ln_b, ret_gn_g, ret_gn_b, w_out[0],
                ln1_g, ln1_b, tabs)
    return _attn_mlp(x1, k, v, w_xq[0], w_xo[0], ln2_g, ln2_b, w_up[0], w_down[0], ln3_g, ln3_b)
```

```python
import functools

import jax
import jax.numpy as jnp
from jax import lax
from jax.experimental import pallas as pl
from jax.experimental.pallas import tpu as pltpu

D_MODEL = 1024
N_MEM = 256
CONV_WIDTH = 512
CONV_KERNEL = 31
RET_WIDTH = 512
RET_HEADS = 4
RET_HEAD_DIM = 128
RET_CHUNK = 128
ROPE_BASE = 10000.0
IN_COLS = 2 * CONV_WIDTH + 4 * RET_WIDTH
XATTN_HEADS = 4
XATTN_HEAD_DIM = 256
D_FF = 4 * D_MODEL
LN_EPS = 1e-5
DEPTH = 1
DN_ALPHA = (2.0 * DEPTH) ** 0.25

CONV_HALO = 32
SUBLANES, LANES = 8, 128
CONV_ROWS = 128
N_STRIPS = CONV_WIDTH // LANES
PROJ_COLS = 512
N_PIECES = IN_COLS // PROJ_COLS
P_A, P_B, P_Q, P_K, P_V, P_G = range(N_PIECES)
MIXER_TILE = 512
ATTN_TILE = 512
KV_BATCHES = 4
FF_CHUNK = 1024
TAIL_SPLIT = 1
VMEM_LIMIT_BYTES = 56 * 1024 * 1024
STAGE_BYTES = 1024 * 1024

BF16 = jnp.bfloat16
F32 = jnp.float32


def _dot(a, b):
    return jnp.dot(a, b, preferred_element_type=F32)


def _dot_nt(a, b):
    return lax.dot_general(a, b, (((1,), (1,)), ((), ())), preferred_element_type=F32)


def _sigmoid(x):
    return 1.0 / (1.0 + jnp.exp(-x))


def _layer_norm(x, g, b):
    mu = jnp.mean(x, axis=-1, keepdims=True)
    xc = x - mu
    var = jnp.mean(xc * xc, axis=-1, keepdims=True)
    return xc * lax.rsqrt(var + LN_EPS) * g + b


def _stage_shape(cols):
    rows = 1 << ((STAGE_BYTES // (4 * cols)).bit_length() - 1)
    return (2, rows, cols)


def _stage_weight_to_bf16(w_hbm_ref, w_bf16_ref, stage_ref, sem_ref):
    rows = stage_ref.shape[1]
    n_chunks = w_hbm_ref.shape[0] // rows
    assert n_chunks * rows == w_hbm_ref.shape[0] and stage_ref.shape[2] == w_hbm_ref.shape[1]

    def chunk_copy(i, slot):
        return pltpu.make_async_copy(w_hbm_ref.at[pl.ds(i * rows, rows)], stage_ref.at[slot], sem_ref.at[slot])

    chunk_copy(0, 0).start()

    def body(i, carry):
        slot = lax.rem(i, 2)

        @pl.when(i + 1 < n_chunks)
        def _():
            chunk_copy(i + 1, 1 - slot).start()

        chunk_copy(i, slot).wait()
        w_bf16_ref[pl.ds(pl.multiple_of(i * rows, rows), rows), :] = stage_ref[slot].astype(BF16)
        return carry

    lax.fori_loop(0, n_chunks, body, 0)


def _kv_kernel(mem_ref, wk_ref, wv_ref, k_ref, v_ref, wk_bf_ref, wv_bf_ref):
    @pl.when(pl.program_id(0) == 0)
    def _():
        wk_bf_ref[...] = wk_ref[...].astype(BF16)
        wv_bf_ref[...] = wv_ref[...].astype(BF16)

    m = mem_ref[...].reshape(KV_BATCHES * N_MEM, D_MODEL).astype(BF16)
    k_ref[...] = _dot(m, wk_bf_ref[...]).astype(BF16).reshape(k_ref.shape)
    v_ref[...] = _dot(m, wv_bf_ref[...]).astype(BF16).reshape(v_ref.shape)


def _kv_proj(mem, w_xk, w_xv):
    B = mem.shape[0]
    const = lambda b: (0, 0)
    batches = lambda: pl.BlockSpec((KV_BATCHES, N_MEM, D_MODEL), lambda b: (b, 0, 0))
    return pl.pallas_call(
        _kv_kernel,
        grid=(B // KV_BATCHES,),
        in_specs=[
            batches(),
            pl.BlockSpec((D_MODEL, D_MODEL), const),
            pl.BlockSpec((D_MODEL, D_MODEL), const),
        ],
        out_specs=[batches(), batches()],
        out_shape=[jax.ShapeDtypeStruct((B, N_MEM, D_MODEL), BF16)] * 2,
        scratch_shapes=[pltpu.VMEM((D_MODEL, D_MODEL), BF16)] * 2,
        compiler_params=pltpu.CompilerParams(dimension_semantics=("arbitrary",),
                                             vmem_limit_bytes=VMEM_LIMIT_BYTES),
        name="kv_proj",
    )(mem, w_xk, w_xv)


def _mixer_step(h_next_ref, h_ref, x_keep_ref, x_ref, seq_tile, w_in_ref, w_out_ref, x_next_ref, conv_w_ref,
                conv_b_ref, cln_g_ref, cln_b_ref, gn_g_ref, gn_b_ref, ln1_g_ref, ln1_b_ref, cos_ref, sin_ref,
                decay_ref, qdec_ref, kdec_ref, cdec_ref, o_ref, xb_ref, ubuf_ref, cbuf_ref, state_ref):
    T = MIXER_TILE

    def project(p):
        h_next_ref[p] = _dot(xb_ref[...], w_in_ref[:, p * PROJ_COLS:(p + 1) * PROJ_COLS])

    def strip_rows(slab_parity, t_start, n):
        slab, parity = slab_parity
        return slab, pl.ds(2 * t_start + parity, n, stride=2), slice(None)

    base = CONV_HALO - (CONV_KERNEL - 1)

    def conv_strip(c):
        where = divmod(c, 2)
        cols = slice(c * LANES, (c + 1) * LANES)
        n_acc = CONV_ROWS // SUBLANES
        for t0 in range(0, T, CONV_ROWS):
            acc = [jnp.broadcast_to(conv_b_ref[:, cols], (SUBLANES, LANES))] * n_acc
            for r in range(SUBLANES):
                taps = range(r, CONV_KERNEL, SUBLANES)
                wts = [jnp.broadcast_to(conv_w_ref[j:j + 1, cols], (SUBLANES, LANES)) for j in taps]
                for m in range(n_acc + len(taps) - 1):
                    win = ubuf_ref[strip_rows(where, t0 + base + r + SUBLANES * m, SUBLANES)]
                    for a, wt in enumerate(wts):
                        if 0 <= m - a < n_acc:
                            acc[m - a] = acc[m - a] + win * wt
            cbuf_ref[c, t0:t0 + CONV_ROWS, :] = jnp.concatenate(acc, axis=0)

    for c in range(N_STRIPS):
        pl.when(pl.program_id(0) >= 0)(functools.partial(conv_strip, c))
    ubuf_ref[:, 0:2 * CONV_HALO, :] = ubuf_ref[:, 2 * T:2 * (T + CONV_HALO), :]
    x_next = x_next_ref[0]
    xb_ref[...] = x_next.astype(BF16)
    project(P_A)
    x_keep_ref[...] = x_next
    project(P_B)
    u_next = h_next_ref[P_A] * _sigmoid(h_next_ref[P_B])
    for c in range(N_STRIPS):
        ubuf_ref[strip_rows(divmod(c, 2), CONV_HALO, T)] = u_next[:, c * LANES:(c + 1) * LANES]
    conv = jnp.concatenate([cbuf_ref[c] for c in range(N_STRIPS)], axis=1)
    cn = _layer_norm(conv, cln_g_ref[...], cln_b_ref[...])
    conv_out = (cn * _sigmoid(cn)).astype(BF16)
    mix_conv = _dot(conv_out, w_out_ref[0:CONV_WIDTH, :])

    pos_rows = pl.ds(pl.multiple_of(seq_tile * T, T), T)
    cos, sin = cos_ref[pos_rows, :], sin_ref[pos_rows, :]
    half = RET_HEAD_DIM // 2
    C = RET_CHUNK
    n_chunks = T // C
    head_cols = [slice(h * RET_HEAD_DIM, (h + 1) * RET_HEAD_DIM) for h in range(RET_HEADS)]
    q, v, scores, kv = [], [], [], []
    project(P_Q)
    for h, hs in enumerate(head_cols):
        qh = h_ref[P_Q, :, hs]
        kh = h_ref[P_K, :, hs]
        qh = qh * cos + pltpu.roll(qh, half, 1) * sin
        kh = kh * cos + pltpu.roll(kh, half, 1) * sin
        vh = h_ref[P_V, :, hs].astype(BF16)
        q.append(qh)
        v.append(vh)
        decay, kdec = decay_ref[h], kdec_ref[h]
        scores_h, kv_h = [], []
        for c in range(n_chunks):
            rows = slice(c * C, (c + 1) * C)
            scores_h.append((_dot_nt(qh[rows].astype(BF16), kh[rows].astype(BF16)) * decay).astype(BF16))
            kv_h.append(_dot((kh[rows] * kdec).T.astype(BF16), vh[rows]))
        scores.append(scores_h)
        kv.append(kv_h)
    project(P_K)
    ret_heads = []
    for h, hs in enumerate(head_cols):
        qdec, cdec = qdec_ref[h], cdec_ref[h]
        state = state_ref[h]
        ys = []
        for c in range(n_chunks):
            rows = slice(c * C, (c + 1) * C)
            lhs = jnp.concatenate([scores[h][c], (q[h][rows] * qdec).astype(BF16)], axis=1)
            rhs = jnp.concatenate([v[h][rows], state.astype(BF16)], axis=0)
            ys.append(_dot(lhs, rhs))
            state = state * cdec + kv[h][c]
        state_ref[h] = state
        y = jnp.concatenate(ys, axis=0)
        mu = jnp.mean(y, axis=-1, keepdims=True)
        yc = y - mu
        var = jnp.mean(yc * yc, axis=-1, keepdims=True)
        yn = yc * lax.rsqrt(var + LN_EPS)
        yn = yn * gn_g_ref[:, hs] + gn_b_ref[:, hs]
        gh = h_ref[P_G, :, hs]
        ret_heads.append((gh * _sigmoid(gh) * yn).astype(BF16))

    project(P_V)
    ret_out = jnp.concatenate(ret_heads, axis=1)
    mix = [mix_conv[r0:r0 + T // TAIL_SPLIT] + _dot(ret_out[r0:r0 + T // TAIL_SPLIT], w_out_ref[CONV_WIDTH:, :])
           for r0 in range(0, T, T // TAIL_SPLIT)]
    project(P_G)
    for i, mix_r in enumerate(mix):
        rows = slice(i * (T // TAIL_SPLIT), (i + 1) * (T // TAIL_SPLIT))
        o_ref[0, rows, :] = _layer_norm(DN_ALPHA * x_ref[rows, :] + mix_r, ln1_g_ref[...], ln1_b_ref[...])


def _mixer_kernel(tiles_per_seq, *refs):
    (w_in_hbm, w_out_hbm, *io_refs, h_even_ref, h_odd_ref, x_even_ref, x_odd_ref, xb_ref, ubuf_ref, cbuf_ref,
     state_ref, w_in_ref, w_out_ref, stage_in_ref, stage_out_ref, sem_ref) = refs
    n = pl.program_id(0)
    seq_tile = lax.rem(jnp.maximum(n - 1, 0), tiles_per_seq)

    @pl.when(n == 0)
    def _():
        _stage_weight_to_bf16(w_in_hbm, w_in_ref, stage_in_ref, sem_ref)
        _stage_weight_to_bf16(w_out_hbm, w_out_ref, stage_out_ref, sem_ref)
        h_odd_ref[...] = jnp.zeros(h_odd_ref.shape, F32)
        x_odd_ref[...] = jnp.zeros(x_odd_ref.shape, F32)
        ubuf_ref[...] = jnp.zeros(ubuf_ref.shape, F32)

    @pl.when(seq_tile == 0)
    def _():
        ubuf_ref[:, 0:2 * CONV_HALO, :] = jnp.zeros((ubuf_ref.shape[0], 2 * CONV_HALO, LANES), F32)
        state_ref[...] = jnp.zeros(state_ref.shape, F32)

    def step(h_next_ref, h_ref, x_keep_ref, x_ref):
        _mixer_step(h_next_ref, h_ref, x_keep_ref, x_ref, seq_tile, w_in_ref, w_out_ref, *io_refs, xb_ref,
                    ubuf_ref, cbuf_ref, state_ref)

    @pl.when(lax.rem(n, 2) == 0)
    def _():
        step(h_even_ref, h_odd_ref, x_even_ref, x_odd_ref)

    @pl.when(lax.rem(n, 2) == 1)
    def _():
        step(h_odd_ref, h_even_ref, x_odd_ref, x_even_ref)


def _mixer(x, w_in, conv_w, conv_b, cln_g, cln_b, gn_g, gn_b, w_out, ln1_g, ln1_b, tabs):
    B, S, D = x.shape
    T = MIXER_TILE
    tiles_per_seq = S // T
    n_tiles = B * tiles_per_seq

    def tile_index(tile):
        return lax.div(tile, tiles_per_seq), lax.rem(tile, tiles_per_seq)

    def next_tile(n):
        b, s = tile_index(jnp.minimum(n, n_tiles - 1))
        return b, s, 0

    def this_tile(n):
        b, s = tile_index(jnp.maximum(n - 1, 0))
        return b, s, 0

    const2 = lambda n: (0, 0)
    const3 = lambda n: (0, 0, 0)
    row = lambda width: pl.BlockSpec((1, width), const2)
    pos = lambda: pl.BlockSpec((S, RET_HEAD_DIM), const2, pipeline_mode=pl.Buffered(1))
    tab = lambda: pl.BlockSpec((RET_HEADS, RET_CHUNK, RET_HEAD_DIM), const3)
    return pl.pallas_call(
        functools.partial(_mixer_kernel, tiles_per_seq),
        grid=(n_tiles + 1,),
        in_specs=[
            pl.BlockSpec(memory_space=pl.ANY), pl.BlockSpec(memory_space=pl.ANY),
            pl.BlockSpec((1, T, D), next_tile),
            pl.BlockSpec((CONV_KERNEL, CONV_WIDTH), const2),
            row(CONV_WIDTH), row(CONV_WIDTH), row(CONV_WIDTH), row(RET_WIDTH), row(RET_WIDTH),
            row(D), row(D),
            pos(), pos(),
            tab(), tab(), tab(), tab(),
        ],
        out_specs=pl.BlockSpec((1, T, D), this_tile),
        out_shape=jax.ShapeDtypeStruct((B, S, D), F32),
        scratch_shapes=[
            pltpu.VMEM((N_PIECES, T, PROJ_COLS), F32),
            pltpu.VMEM((N_PIECES, T, PROJ_COLS), F32),
            pltpu.VMEM((T, D), F32),
            pltpu.VMEM((T, D), F32),
            pltpu.VMEM((T, D), BF16),
            pltpu.VMEM((N_STRIPS // 2, 2 * (CONV_HALO + T), LANES), F32),
            pltpu.VMEM((N_STRIPS, T, LANES), F32),
            pltpu.VMEM((RET_HEADS, RET_HEAD_DIM, RET_HEAD_DIM), F32),
            pltpu.VMEM((D, IN_COLS), BF16),
            pltpu.VMEM((D, D), BF16),
            pltpu.VMEM(_stage_shape(IN_COLS), F32),
            pltpu.VMEM(_stage_shape(D), F32),
            pltpu.SemaphoreType.DMA((2,)),
        ],
        compiler_params=pltpu.CompilerParams(dimension_semantics=("arbitrary",),
                                             vmem_limit_bytes=VMEM_LIMIT_BYTES),
        name="mixer",
    )(w_in, w_out, x, conv_w, conv_b, cln_g, cln_b, gn_g, gn_b, ln1_g, ln1_b, *tabs)


def _attn_mlp_kernel(x_ref, k_ref, v_ref, wq_hbm, wo_hbm, ln2_g_ref, ln2_b_ref, wup_hbm, wdn_hbm,
                     ln3_g_ref, ln3_b_ref, o_ref, wq_ref, wo_ref, wup_ref, wdn_ref, stage_d_ref, stage_ff_ref,
                     sem_ref):
    @pl.when((pl.program_id(0) == 0) & (pl.program_id(1) == 0))
    def _():
        _stage_weight_to_bf16(wq_hbm, wq_ref, stage_d_ref, sem_ref)
        _stage_weight_to_bf16(wo_hbm, wo_ref, stage_d_ref, sem_ref)
        _stage_weight_to_bf16(wup_hbm, wup_ref, stage_ff_ref, sem_ref)
        _stage_weight_to_bf16(wdn_hbm, wdn_ref, stage_d_ref, sem_ref)

    x = x_ref[0]
    T = x.shape[0]
    halves = (slice(0, T // 2), slice(T // 2, T))
    q = (_dot(x.astype(BF16), wq_ref[...]) * (XATTN_HEAD_DIM ** -0.5)).astype(BF16)
    head_cols = [slice(h * XATTN_HEAD_DIM, (h + 1) * XATTN_HEAD_DIM) for h in range(XATTN_HEADS)]
    scores = [_dot_nt(q[:, hs], k_ref[0, :, hs]) for hs in head_cols]
    heads = []
    for s, hs in zip(scores, head_cols):
        p = jnp.exp(s - jnp.max(s, axis=-1, keepdims=True))
        l = jnp.sum(p, axis=-1, keepdims=True)
        o = _dot(p.astype(BF16), v_ref[0, :, hs])
        heads.append((o * (1.0 / l)).astype(BF16))
    attn = jnp.concatenate(heads, axis=1)
    xa = [_dot(attn[rows], wo_ref[...]) for rows in halves]
    x2 = [_layer_norm(DN_ALPHA * x[rows] + xa_r, ln2_g_ref[...], ln2_b_ref[...]) for rows, xa_r in zip(halves, xa)]
    x2b_halves = [x2_r.astype(BF16) for x2_r in x2]
    n_chunks = D_FF // FF_CHUNK
    up_cols = [slice(j * FF_CHUNK, (j + 1) * FF_CHUNK) for j in range(n_chunks)]

    def act(up):
        r = jnp.maximum(up, 0.0)
        return (r * r).astype(BF16)

    up = jnp.concatenate([_dot(x2b_r, wup_ref[:, up_cols[0]]) for x2b_r in x2b_halves], axis=0)
    x2b = jnp.concatenate(x2b_halves, axis=0)
    ff = None
    for j in range(1, n_chunks):
        up_next = _dot(x2b, wup_ref[:, up_cols[j]])
        part = _dot(act(up), wdn_ref[up_cols[j - 1], :])
        ff = part if ff is None else ff + part
        up = up_next
    hid = act(up)
    for rows, x2_r in zip(halves, x2):
        ff_r = ff[rows] + _dot(hid[rows], wdn_ref[up_cols[-1], :])
        o_ref[0, rows, :] = _layer_norm(DN_ALPHA * x2_r + ff_r, ln3_g_ref[...], ln3_b_ref[...])


def _attn_mlp(x1, k, v, w_xq, w_xo, ln2_g, ln2_b, w_up, w_down, ln3_g, ln3_b):
    B, S, D = x1.shape
    T = ATTN_TILE
    const2 = lambda b, s: (0, 0)
    row = lambda n: pl.BlockSpec((1, n), const2)
    hbm = pl.BlockSpec(memory_space=pl.ANY)
    return pl.pallas_call(
        _attn_mlp_kernel,
        grid=(B, S // T),
        in_specs=[
            pl.BlockSpec((1, T, D), lambda b, s: (b, s, 0)),
            pl.BlockSpec((1, N_MEM, D), lambda b, s: (b, 0, 0)),
            pl.BlockSpec((1, N_MEM, D), lambda b, s: (b, 0, 0)),
            hbm, hbm,
            row(D), row(D),
            hbm, hbm,
            row(D), row(D),
        ],
        out_specs=pl.BlockSpec((1, T, D), lambda b, s: (b, s, 0)),
        out_shape=jax.ShapeDtypeStruct((B, S, D), F32),
        scratch_shapes=[
            pltpu.VMEM((D, D), BF16), pltpu.VMEM((D, D), BF16),
            pltpu.VMEM((D, D_FF), BF16), pltpu.VMEM((D_FF, D), BF16),
            pltpu.VMEM(_stage_shape(D), F32),
            pltpu.VMEM(_stage_shape(D_FF), F32),
            pltpu.SemaphoreType.DMA((2,)),
        ],
        compiler_params=pltpu.CompilerParams(dimension_semantics=("arbitrary", "arbitrary"),
                                             vmem_limit_bytes=VMEM_LIMIT_BYTES),
        name="attn_mlp",
    )(x1, k, v, w_xq, w_xo, ln2_g, ln2_b, w_up, w_down, ln3_g, ln3_b)


def _position_tables(seq):
    half = RET_HEAD_DIM // 2
    pos = jnp.arange(seq, dtype=F32)
    inv = ROPE_BASE ** (-jnp.linspace(0.0, 1.0, half, dtype=F32))
    ang = pos[:, None] * inv[None, :]
    cos, sin = jnp.cos(ang), jnp.sin(ang)
    cos2 = jnp.concatenate([cos, cos], axis=-1)
    sin2 = jnp.concatenate([-sin, sin], axis=-1)
    qs = RET_HEAD_DIM ** -0.5
    C = RET_CHUNK
    log_g = jnp.log(1.0 - 2.0 ** (-5.0 - jnp.arange(RET_HEADS, dtype=F32)))
    idx = jnp.arange(C, dtype=F32)
    rel = idx[:, None] - idx[None, :]
    decay = jnp.where(rel >= 0, jnp.exp(log_g[:, None, None] * jnp.maximum(rel, 0.0)), 0.0)
    q_dec = jnp.exp(log_g[:, None] * (idx + 1.0))
    k_dec = jnp.exp(log_g[:, None] * (C - 1.0 - idx))
    chunk_dec = jnp.exp(log_g * C)
    bcast = lambda t: jnp.broadcast_to(t, (RET_HEADS, C, RET_HEAD_DIM))
    return (cos2, sin2, decay * qs,
            bcast(q_dec[:, :, None]) * qs, bcast(k_dec[:, :, None]), bcast(chunk_dec[:, None, None]))


def kernel(x, mem, w_in, conv_w, conv_b, conv_ln_g, conv_ln_b, ret_gn_g, ret_gn_b, w_out,
           ln1_g, ln1_b, w_xq, w_xk, w_xv, w_xo, ln2_g, ln2_b, w_up, w_down, ln3_g, ln3_b):
    assert w_in.shape[0] == DEPTH == 1
    assert RET_CHUNK == RET_HEAD_DIM
    with jax.ensure_compile_time_eval():
        tabs = _position_tables(x.shape[1])
    k, v = _kv_proj(mem, w_xk[0], w_xv[0])
    x1 = _mixer(x, w_in[0], conv_w[0], conv_b, conv_ln_g, conv_ln_b, ret_gn_g, ret_gn_b, w_out[0],
                ln1_g, ln1_b, tabs)
    return _attn_mlp(x1, k, v, w_xq[0], w_xo[0], ln2_g, ln2_b, w_up[0], w_down[0], ln3_g, ln3_b)
```
